```python
import math
import jax, jax.numpy as jnp
from jax import lax
import numpy as np

D_MODEL = 4096
BATCH = 2
SEQ = 4096
DEPTH = 2

N_A_LAYERS = DEPTH // 2
N_B_LAYERS = DEPTH - N_A_LAYERS
N_HEADS = 32
HEAD_DIM = D_MODEL // N_HEADS
NSA_KV_HEADS = 4
NSA_CMP_LEN = 32
NSA_CMP_STRIDE = 16
NSA_SEL_BLOCK = 64
NSA_SEL_TOPN = 16
NSA_WINDOW = 512
NSA_Q_CHUNK = 64
NSA_N_BRANCH = 3
NSA_IN_WIDTH = (N_HEADS * HEAD_DIM + NSA_N_BRANCH * 2 * NSA_KV_HEADS * HEAD_DIM
                + NSA_N_BRANCH * N_HEADS * HEAD_DIM + NSA_N_BRANCH * N_HEADS)
MOBA_KV_HEADS = 8
MOBA_BLOCK = 256
MOBA_TOPK = 3
MOBA_Q_CHUNK = 16
RMS_EPS = 1e-6
NEG_INF = -1e30
FORCE_SCORE = 1e9

kernel_name = "yoco_nsa_moba_alibi_hybrid"


def rmsnorm(x, g):
    xf = x.astype(jnp.float32)
    y = xf * lax.rsqrt(jnp.mean(xf * xf, axis=-1, keepdims=True) + RMS_EPS)
    return (y * g.astype(jnp.float32)).astype(x.dtype)


def alibi_slopes(n_heads):
    return 2.0 ** (-8.0 * jnp.arange(1, n_heads + 1, dtype=jnp.float32) / n_heads)


def nsa_mixer(h, norm_g, w_in, pos_k, pos_v, w1_k, w2_k, w1_v, w2_v, w_out):
    B, S, _ = h.shape
    H, Dh, G = N_HEADS, HEAD_DIM, NSA_KV_HEADS
    hpg = H // G
    L, d, Ls, W, Cq = NSA_CMP_LEN, NSA_CMP_STRIDE, NSA_SEL_BLOCK, NSA_WINDOW, NSA_Q_CHUNK
    xn = rmsnorm(h, norm_g)
    proj = xn @ w_in
    q_end = H * Dh
    kv_end = q_end + NSA_N_BRANCH * 2 * G * Dh
    z_end = kv_end + NSA_N_BRANCH * H * Dh
    q = proj[..., :q_end].reshape(B, S, G, hpg, Dh).transpose(0, 2, 3, 1, 4) * (Dh ** -0.5)
    kv = proj[..., q_end:kv_end].reshape(B, S, NSA_N_BRANCH, 2, G, Dh).transpose(2, 3, 0, 4, 1, 5)
    z = proj[..., kv_end:z_end].reshape(B, S, NSA_N_BRANCH, H, Dh)
    gates = jax.nn.sigmoid(proj[..., z_end:].astype(jnp.float32)).reshape(B, S, NSA_N_BRANCH, H)
    k_cmp_raw, v_cmp_raw = kv[0, 0], kv[0, 1]
    k_slc, v_slc = kv[1, 0], kv[1, 1]
    k_win, v_win = kv[2, 0], kv[2, 1]

    n_cmp = (S - L) // d + 1
    cmp_idx = jnp.arange(n_cmp)[:, None] * d + jnp.arange(L)[None, :]
    cmp_end = cmp_idx[:, -1]

    def compress(raw, pos, w1, w2):
        blk = raw[:, :, cmp_idx] + pos
        hid = jax.nn.silu(jnp.einsum('bgnld,lde->bgne', blk, w1))
        return hid @ w2

    k_c = compress(k_cmp_raw, pos_k, w1_k, w2_k)
    v_c = compress(v_cmp_raw, pos_v, w1_v, w2_v)

    n_slc = S // Ls
    n_top = min(NSA_SEL_TOPN, n_slc)
    ks_blk = k_slc.reshape(B, G, n_slc, Ls, Dh)
    vs_blk = v_slc.reshape(B, G, n_slc, Ls, Dh)
    slc_start = jnp.arange(n_slc) * Ls
    overlap = ((cmp_idx[:, :1] <= (slc_start + Ls - 1)[None, :]) &
               (cmp_end[:, None] >= slc_start[None, :])).astype(jnp.float32)

    pad = ((0, 0), (0, 0), (W, 0), (0, 0))
    kw_p = jnp.pad(k_win, pad)
    vw_p = jnp.pad(v_win, pad)

    sl = alibi_slopes(H).reshape(G, hpg)[None, :, :, None, None]
    bi = jnp.arange(B)[:, None, None, None]
    gi = jnp.arange(G)[None, :, None, None]
    j_slc = jnp.arange(n_slc)

    def chunk(c):
        t0 = c * Cq
        tq = t0 + jnp.arange(Cq)
        qc = lax.dynamic_slice_in_dim(q, t0, Cq, axis=3)
        dist_c = (tq[:, None] - cmp_end[None, :]).astype(jnp.float32)
        mask_c = dist_c >= 0
        s_c = jnp.einsum('bghqd,bgnd->bghqn', qc, k_c).astype(jnp.float32) - sl * dist_c
        p_c = jnp.where(mask_c, jax.nn.softmax(jnp.where(mask_c, s_c, NEG_INF), axis=-1), 0.0)
        o_c = jnp.einsum('bghqn,bgnd->bghqd', p_c.astype(v_c.dtype), v_c)
        imp = jnp.einsum('bghqn,nj->bgqj', p_c, overlap)
        blk_q = (tq // Ls)[:, None]
        forced = (j_slc[None, :] == 0) | (j_slc[None, :] == blk_q) | (j_slc[None, :] == blk_q - 1)
        imp = jnp.where(forced, FORCE_SCORE, jnp.where(j_slc[None, :] > blk_q, NEG_INF, imp))
        _, sel = lax.top_k(imp, n_top)
        kg = ks_blk[bi, gi, sel].reshape(B, G, Cq, n_top * Ls, Dh)
        vg = vs_blk[bi, gi, sel].reshape(B, G, Cq, n_top * Ls, Dh)
        pos_s = (sel[..., None] * Ls + jnp.arange(Ls)).reshape(B, G, Cq, n_top * Ls)
        dist_s = (tq[None, None, :, None] - pos_s)[:, :, None].astype(jnp.float32)
        mask_s = dist_s >= 0
        s_s = jnp.einsum('bghqd,bgqkd->bghqk', qc, kg).astype(jnp.float32) - sl * dist_s
        p_s = jax.nn.softmax(jnp.where(mask_s, s_s, NEG_INF), axis=-1)
        o_s = jnp.einsum('bghqk,bgqkd->bghqd', p_s.astype(vg.dtype), vg)
        kw = lax.dynamic_slice_in_dim(kw_p, t0, W + Cq, axis=2)
        vw = lax.dynamic_slice_in_dim(vw_p, t0, W + Cq, axis=2)
        pos_w = t0 - W + jnp.arange(W + Cq)
        dist_w = tq[:, None] - pos_w[None, :]
        mask_w = (dist_w >= 0) & (dist_w < W)
        s_w = jnp.einsum('bghqd,bgkd->bghqk', qc, kw).astype(jnp.float32) - sl * dist_w.astype(jnp.float32)
        p_w = jax.nn.softmax(jnp.where(mask_w, s_w, NEG_INF), axis=-1)
        o_w = jnp.einsum('bghqk,bgkd->bghqd', p_w.astype(vw.dtype), vw)
        return jnp.stack([o_c, o_s, o_w], axis=0)

    outs = lax.map(chunk, jnp.arange(S // Cq))
    O = outs.transpose(2, 0, 5, 1, 3, 4, 6).reshape(B, S, NSA_N_BRANCH, H, Dh)
    mix = jnp.sum(gates.astype(O.dtype)[..., None] * O * jax.nn.silu(z), axis=2)
    return mix.reshape(B, S, H * Dh) @ w_out


def moba_shared_kv(h, norm_g, w_kv):
    B, S, _ = h.shape
    Gm, Dh, Bk = MOBA_KV_HEADS, HEAD_DIM, MOBA_BLOCK
    nb = -(-S // Bk)
    kv = (rmsnorm(h, norm_g) @ w_kv).reshape(B, S, 2, Gm, Dh).transpose(2, 0, 3, 1, 4)
    kv = jnp.pad(kv, ((0, 0), (0, 0), (0, 0), (0, nb * Bk - S), (0, 0)))
    kb = kv[0].reshape(B, Gm, nb, Bk, Dh)
    vb = kv[1].reshape(B, Gm, nb, Bk, Dh)
    kmean = jnp.mean(kb.astype(jnp.float32), axis=3).astype(kb.dtype)
    return kb, vb, kmean


def moba_mixer(h, norm_g, w_in, w_out, kb, vb, kmean):
    B, S, _ = h.shape
    H, Dh, Gm, Bk, Cq = N_HEADS, HEAD_DIM, MOBA_KV_HEADS, MOBA_BLOCK, MOBA_Q_CHUNK
    hpg = H // Gm
    nb = kb.shape[2]
    proj = rmsnorm(h, norm_g) @ w_in
    q = proj[..., :H * Dh].reshape(B, S, Gm, hpg, Dh).transpose(0, 2, 3, 1, 4) * (Dh ** -0.5)
    z = proj[..., H * Dh:].reshape(B, S, H, Dh)
    blk_t = jnp.arange(S) // Bk
    past = jnp.arange(nb)[None, :] < blk_t[:, None]
    s_blk = jnp.einsum('bghsd,bgnd->bghsn', q, kmean).astype(jnp.float32)
    s_blk = jnp.where(past, s_blk, NEG_INF)
    _, top_idx = lax.top_k(s_blk, min(MOBA_TOPK, nb))
    top_ok = top_idx < blk_t[:, None]
    k_top = top_idx.shape[-1]

    sm = alibi_slopes(H).reshape(Gm, hpg)[None, :, :, None, None]
    bi = jnp.arange(B)[:, None, None, None, None]
    gi = jnp.arange(Gm)[None, :, None, None, None]

    def chunk(c):
        t0 = c * Cq
        tq = t0 + jnp.arange(Cq)
        qc = lax.dynamic_slice_in_dim(q, t0, Cq, axis=3)
        idx = lax.dynamic_slice_in_dim(top_idx, t0, Cq, axis=3)
        ok = lax.dynamic_slice_in_dim(top_ok, t0, Cq, axis=3)
        own = t0 // Bk
        k_own = lax.dynamic_index_in_dim(kb, own, axis=2, keepdims=False)
        v_own = lax.dynamic_index_in_dim(vb, own, axis=2, keepdims=False)
        dist_o = tq[:, None] - (own * Bk + jnp.arange(Bk))[None, :]
        s_o = jnp.einsum('bghqd,bgkd->bghqk', qc, k_own).astype(jnp.float32) - sm * dist_o.astype(jnp.float32)
        s_o = jnp.where(dist_o >= 0, s_o, NEG_INF)
        kg = kb[bi, gi, idx].reshape(B, Gm, hpg, Cq, k_top * Bk, Dh)
        vg = vb[bi, gi, idx].reshape(B, Gm, hpg, Cq, k_top * Bk, Dh)
        pos_g = (idx[..., None] * Bk + jnp.arange(Bk)).reshape(B, Gm, hpg, Cq, k_top * Bk)
        ok_g = jnp.broadcast_to(ok[..., None], idx.shape + (Bk,)).reshape(B, Gm, hpg, Cq, k_top * Bk)
        dist_g = (tq[:, None] - pos_g).astype(jnp.float32)
        s_g = jnp.einsum('bghqd,bghqkd->bghqk', qc, kg).astype(jnp.float32) - sm * dist_g
        s_g = jnp.where(ok_g, s_g, NEG_INF)
        p = jax.nn.softmax(jnp.concatenate([s_o, s_g], axis=-1), axis=-1)
        p_o, p_g = p[..., :Bk], p[..., Bk:]
        return (jnp.einsum('bghqk,bgkd->bghqd', p_o.astype(v_own.dtype), v_own)
                + jnp.einsum('bghqk,bghqkd->bghqd', p_g.astype(vg.dtype), vg))

    outs = lax.map(chunk, jnp.arange(S // Cq))
    o = outs.transpose(1, 0, 4, 2, 3, 5).reshape(B, S, H, Dh)
    return (o * jax.nn.silu(z)).reshape(B, S, H * Dh) @ w_out


def setup_inputs(seed: int = 0) -> dict:
    key = jax.random.key(seed)
    ks = jax.random.split(key, 20)
    D, H, Dh, L = D_MODEL, N_HEADS, HEAD_DIM, NSA_CMP_LEN
    nA, nB = N_A_LAYERS, N_B_LAYERS
    f32 = jnp.float32

    def nrm(k, shape, scale):
        return jax.random.normal(k, shape, f32) * scale

    return {
        "x": nrm(ks[0], (BATCH, SEQ, D), 1.0),
        "a_norm_g": 1.0 + nrm(ks[1], (nA, D), 0.01),
        "a_w_in": nrm(ks[2], (nA, D, NSA_IN_WIDTH), D ** -0.5),
        "a_cmp_pos_k": nrm(ks[3], (nA, L, Dh), 0.02),
        "a_cmp_pos_v": nrm(ks[4], (nA, L, Dh), 0.02),
        "a_cmp_w1_k": nrm(ks[5], (nA, L, Dh, Dh), (L * Dh) ** -0.5),
        "a_cmp_w2_k": nrm(ks[6], (nA, Dh, Dh), Dh ** -0.5),
        "a_cmp_w1_v": nrm(ks[7], (nA, L, Dh, Dh), (L * Dh) ** -0.5),
        "a_cmp_w2_v": nrm(ks[8], (nA, Dh, Dh), Dh ** -0.5),
        "a_w_out": nrm(ks[9], (nA, H * Dh, D), (H * Dh) ** -0.5),
        "kv_norm_g": 1.0 + nrm(ks[10], (D,), 0.01),
        "kv_w": nrm(ks[11], (D, 2 * MOBA_KV_HEADS * Dh), D ** -0.5),
        "b_norm_g": 1.0 + nrm(ks[12], (nB, D), 0.01),
        "b_w_in": nrm(ks[13], (nB, D, 2 * H * Dh), D ** -0.5),
        "b_w_out": nrm(ks[14], (nB, H * Dh, D), (H * Dh) ** -0.5),
        "final_norm_g": 1.0 + nrm(ks[15], (D,), 0.01),
    }


def reference(x, a_norm_g, a_w_in, a_cmp_pos_k, a_cmp_pos_v, a_cmp_w1_k, a_cmp_w2_k,
              a_cmp_w1_v, a_cmp_w2_v, a_w_out, kv_norm_g, kv_w, b_norm_g, b_w_in,
              b_w_out, final_norm_g):
    h = x
    kb = vb = kmean = None
    for layer in range(DEPTH):
        if layer < N_A_LAYERS:
            h = h + nsa_mixer(h, a_norm_g[layer], a_w_in[layer], a_cmp_pos_k[layer],
                              a_cmp_pos_v[layer], a_cmp_w1_k[layer], a_cmp_w2_k[layer],
                              a_cmp_w1_v[layer], a_cmp_w2_v[layer], a_w_out[layer])
        else:
            if layer == N_A_LAYERS:
                kb, vb, kmean = moba_shared_kv(h, kv_norm_g, kv_w)
            i = layer - N_A_LAYERS
            h = h + moba_mixer(h, b_norm_g[i], b_w_in[i], b_w_out[i], kb, vb, kmean)
    return rmsnorm(h, final_norm_g)
```

```python
import functools

import jax
import jax.numpy as jnp
from jax import lax
from jax.experimental import pallas as pl
from jax.experimental.pallas import tpu as pltpu

F32 = jnp.float32
BF16 = jnp.bfloat16

N_HEADS = 32
HEAD_DIM = 128
NSA_KV_HEADS = 4
NSA_HPG = N_HEADS // NSA_KV_HEADS
NSA_CMP_LEN = 32
NSA_CMP_STRIDE = 16
NSA_SEL_BLOCK = 64
NSA_SEL_TOPN = 16
NSA_WINDOW = 512
MOBA_KV_HEADS = 8
MOBA_HPG = N_HEADS // MOBA_KV_HEADS
MOBA_BLOCK = 256
MOBA_TOPK = 3
RMS_EPS = 1e-6
NEG_INF = -1e30
FORCE_SCORE = 1e9

LANES = 128
VMEM_LIMIT = 56 * 1024 * 1024

_NT = (((1,), (1,)), ((), ()))


def _sigmoid(x):
    return 1.0 / (1.0 + jnp.exp(-x))


def _rmsnorm_kernel(x_ref, g_ref, *o_refs):
    x = x_ref[...].astype(F32)
    y = x * lax.rsqrt(jnp.mean(x * x, axis=-1, keepdims=True) + RMS_EPS)
    for i, o_ref in enumerate(o_refs):
        o_ref[...] = (y * g_ref[i:i + 1, :]).astype(o_ref.dtype)


def _rmsnorm(x, gains, out_dtype):
    m, d = x.shape
    n = gains.shape[0]
    tm = min(256, m)
    outs = pl.pallas_call(
        _rmsnorm_kernel,
        grid=(m // tm,),
        in_specs=[pl.BlockSpec((tm, d), lambda i: (i, 0)),
                  pl.BlockSpec((n, d), lambda i: (0, 0))],
        out_specs=[pl.BlockSpec((tm, d), lambda i: (i, 0)) for _ in range(n)],
        out_shape=[jax.ShapeDtypeStruct((m, d), out_dtype) for _ in range(n)],
        compiler_params=pltpu.CompilerParams(
            dimension_semantics=("arbitrary",), vmem_limit_bytes=VMEM_LIMIT),
        name="rmsnorm",
    )(x, gains)
    return outs


def _mm_scale_kernel(a_ref, w_ref, s_ref, o_ref):
    acc = jnp.dot(a_ref[...], w_ref[...], preferred_element_type=F32)
    o_ref[...] = (acc * s_ref[...]).astype(o_ref.dtype)


def _mm_resid_kernel(a_ref, w_ref, r_ref, o_ref):
    acc = jnp.dot(a_ref[...], w_ref[...], preferred_element_type=F32)
    o_ref[...] = (r_ref[...] + acc).astype(o_ref.dtype)


def _matmul(a, w, *, tn, out_dtype, scale=None, residual=None, name):
    m, k = a.shape
    n = w.shape[1]
    tm = min(1024, m)
    a_spec = pl.BlockSpec((tm, k), lambda i, j: (i, 0))
    w_spec = pl.BlockSpec((k, tn), lambda i, j: (0, j))
    o_spec = pl.BlockSpec((tm, tn), lambda i, j: (i, j))
    if residual is None:
        kern = _mm_scale_kernel
        extra, extra_spec = scale, pl.BlockSpec((1, tn), lambda i, j: (0, j))
    else:
        kern = _mm_resid_kernel
        extra, extra_spec = residual, o_spec
    return pl.pallas_call(
        kern,
        grid=(m // tm, n // tn),
        in_specs=[a_spec, w_spec, extra_spec],
        out_specs=o_spec,
        out_shape=jax.ShapeDtypeStruct((m, n), out_dtype),
        compiler_params=pltpu.CompilerParams(
            dimension_semantics=("arbitrary", "arbitrary"), vmem_limit_bytes=VMEM_LIMIT),
        name=name,
    )(a, w, extra)


def _cmp_kernel(r_ref, pa_ref, pb_ref, w1a_ref, w1b_ref, w2_ref, o_ref):
    r = r_ref[...].astype(F32)
    nr = r.shape[0]
    xa = (r + pa_ref[...]).astype(BF16)
    xb = (r + pb_ref[...]).astype(BF16)
    ya = jnp.dot(xa, w1a_ref[...], preferred_element_type=F32)
    yb = jnp.dot(xb, w1b_ref[...], preferred_element_type=F32)
    hid = ya + pltpu.roll(yb, nr - 1, 0)
    hid = hid * _sigmoid(hid)
    o_ref[...] = jnp.dot(hid.astype(BF16), w2_ref[...], preferred_element_type=F32).astype(o_ref.dtype)


def _compress(r, pa, pb, w1a, w1b, w2):
    two, bg, nr, kk = r.shape
    dh = w2.shape[-1]
    sq = pl.Squeezed()
    wspec = lambda shape: pl.BlockSpec((sq,) + shape, lambda t, i: (t, 0, 0))
    return pl.pallas_call(
        _cmp_kernel,
        grid=(two, bg),
        in_specs=[pl.BlockSpec((sq, sq, nr, kk), lambda t, i: (t, i, 0, 0)),
                  wspec((1, kk)), wspec((1, kk)), wspec((kk, dh)), wspec((kk, dh)), wspec((dh, dh))],
        out_specs=pl.BlockSpec((sq, sq, nr, dh), lambda t, i: (t, i, 0, 0)),
        out_shape=jax.ShapeDtypeStruct((two, bg, nr, dh), BF16),
        compiler_params=pltpu.CompilerParams(
            dimension_semantics=("arbitrary", "arbitrary"), vmem_limit_bytes=VMEM_LIMIT),
        name="nsa_compress",
    )(r, pa, pb, w1a, w1b, w2)


def _stack_heads(ref, n_heads):
    return jnp.concatenate([ref[:, h * HEAD_DIM:(h + 1) * HEAD_DIM] for h in range(n_heads)], axis=0)


def _row_ids(n_heads, tq):
    assert tq & (tq - 1) == 0, "query tile must be a power of two"
    row = lax.broadcasted_iota(jnp.int32, (n_heads * tq, 1), 0)
    return row >> (tq.bit_length() - 1), row & (tq - 1)


def _slope_col(sl_ref, base, n_heads, tq):
    return jnp.concatenate(
        [jnp.full((tq, 1), sl_ref[base + h], F32) for h in range(n_heads)], axis=0)


def _nsa_kernel(sl_ref, q_ref, zc_ref, zs_ref, zw_ref, gt_ref, ks_ref, vs_ref, kw_ref, vw_ref,
                kc_ref, vc_ref, e_ref, ovl_ref, o_ref, bias_ref, m_ref, l_ref, acc_ref,
                *, tq, tk, n_top):
    g = pl.program_id(1)
    i = pl.program_id(2)
    t0 = i * tq
    hpg = NSA_HPG
    rows = hpg * tq
    qa = _stack_heads(q_ref, hpg)
    _, qq = _row_ids(hpg, tq)
    tqv = t0 + qq
    slope = _slope_col(sl_ref, g * hpg, hpg, tq)

    kc = kc_ref[...]
    nc = kc.shape[0]
    s = lax.dot_general(qa, kc, _NT, preferred_element_type=F32)
    cend = lax.broadcasted_iota(jnp.int32, (1, nc), 1) * NSA_CMP_STRIDE + (NSA_CMP_LEN - 1)
    dist = (tqv - cend).astype(F32)
    mask = dist >= 0
    s = jnp.where(mask, s - slope * dist, NEG_INF)
    mx = jnp.max(s, axis=-1, keepdims=True)
    e = jnp.where(mask, jnp.exp(s - mx), 0.0)
    den = jnp.sum(e, axis=-1, keepdims=True)
    p = e / jnp.where(den > 0, den, 1.0)
    o_cmp = jnp.dot(p.astype(BF16), vc_ref[...], preferred_element_type=F32)
    psum = p[0:tq]
    for h in range(1, hpg):
        psum = psum + p[h * tq:(h + 1) * tq]

    p_hi = psum.astype(BF16)
    p_lo = (psum - p_hi.astype(F32)).astype(BF16)
    ovl = ovl_ref[...]
    imp = (lax.dot_general(ovl, p_hi, _NT, preferred_element_type=F32)
           + lax.dot_general(ovl, p_lo, _NT, preferred_element_type=F32))
    nb = imp.shape[0]
    j = lax.broadcasted_iota(jnp.int32, (nb, tq), 0)
    blkq = (t0 + lax.broadcasted_iota(jnp.int32, (nb, tq), 1)) >> (NSA_SEL_BLOCK.bit_length() - 1)
    forced = (j == 0) | (j == blkq) | (j == blkq - 1)
    imp = jnp.where(forced, FORCE_SCORE, jnp.where(j > blkq, NEG_INF, imp))
    rank = jnp.zeros((nb, tq), jnp.int32)
    for ii in range(nb):
        row = imp[ii:ii + 1, :]
        beats = (row > imp) | ((row == imp) & (j > ii))
        rank = rank + beats.astype(jnp.int32)
    selneg = jnp.where(rank < n_top, 0.0, NEG_INF).astype(F32)
    if nb < LANES:
        selneg = jnp.concatenate([selneg, jnp.zeros((LANES - nb, tq), F32)], axis=0)
    selneg_q = selneg.T.astype(BF16)

    nkt = (t0 + tq + tk - 1) // tk
    qpos = t0 + lax.broadcasted_iota(jnp.int32, (tq, 1), 0)

    def bias_body(kt, carry):
        k0 = pl.multiple_of(kt * tk, tk)
        b = jnp.dot(selneg_q, e_ref[:, pl.ds(k0, tk)], preferred_element_type=F32)
        kpos = k0 + lax.broadcasted_iota(jnp.int32, (1, tk), 1)
        bias_ref[:, pl.ds(k0, tk)] = jnp.where(kpos <= qpos, b, NEG_INF)
        return carry

    lax.fori_loop(0, nkt, bias_body, 0)

    m_ref[...] = jnp.full((rows, 1), NEG_INF, F32)
    l_ref[...] = jnp.zeros((rows, 1), F32)
    acc_ref[...] = jnp.zeros((rows, HEAD_DIM), F32)

    def sel_body(kt, carry):
        k0 = pl.multiple_of(kt * tk, tk)
        kt_ = ks_ref[pl.ds(k0, tk), :]
        vt_ = vs_ref[pl.ds(k0, tk), :]
        sc = lax.dot_general(qa, kt_, _NT, preferred_element_type=F32)
        kpos = k0 + lax.broadcasted_iota(jnp.int32, (1, tk), 1)
        dst = (tqv - kpos).astype(F32)
        sc = sc - slope * dst
        b = bias_ref[:, pl.ds(k0, tk)]
        sc = sc + jnp.concatenate([b] * hpg, axis=0)
        m_old = m_ref[...]
        m_new = jnp.maximum(m_old, jnp.max(sc, axis=-1, keepdims=True))
        alpha = jnp.exp(m_old - m_new)
        pr = jnp.exp(sc - m_new)
        l_ref[...] = alpha * l_ref[...] + jnp.sum(pr, axis=-1, keepdims=True)
        acc_ref[...] = alpha * acc_ref[...] + jnp.dot(pr.astype(BF16), vt_, preferred_element_type=F32)
        m_ref[...] = m_new
        return carry

    lax.fori_loop(0, nkt, sel_body, 0)
    o_sel = acc_ref[...] / l_ref[...]

    wlen = NSA_WINDOW + tq
    w0 = pl.multiple_of(t0, tq)
    kwin = kw_ref[pl.ds(w0, wlen), :]
    vwin = vw_ref[pl.ds(w0, wlen), :]
    sw = lax.dot_general(qa, kwin, _NT, preferred_element_type=F32)
    posw = t0 - NSA_WINDOW + lax.broadcasted_iota(jnp.int32, (1, wlen), 1)
    dw = tqv - posw
    mw = (dw >= 0) & (dw < NSA_WINDOW)
    sw = jnp.where(mw, sw - slope * dw.astype(F32), NEG_INF)
    mxw = jnp.max(sw, axis=-1, keepdims=True)
    pw = jnp.exp(sw - mxw)
    o_win = (jnp.dot(pw.astype(BF16), vwin, preferred_element_type=F32)
             / jnp.sum(pw, axis=-1, keepdims=True))

    gts = gt_ref[...].astype(F32)
    gts = pltpu.roll(gts, (LANES - g * hpg) % LANES, 1)
    gts = _sigmoid(gts)
    for h in range(hpg):
        rs = slice(h * tq, (h + 1) * tq)
        cs = slice(h * HEAD_DIM, (h + 1) * HEAD_DIM)
        mix = jnp.zeros((tq, HEAD_DIM), F32)
        for br, (o_br, z_ref) in enumerate(((o_cmp, zc_ref), (o_sel, zs_ref), (o_win, zw_ref))):
            z = z_ref[:, cs].astype(F32)
            gate = gts[:, br * N_HEADS + h:br * N_HEADS + h + 1]
            mix = mix + gate * o_br[rs] * (z * _sigmoid(z))
        o_ref[:, cs] = mix.astype(o_ref.dtype)


def _nsa_attention(slopes, proj, gate_logits, kwp, vwp, kvc, e_mat, ovl_t, *, batch, seq):
    m = proj.shape[0]
    tq = min(128, seq)
    tk = min(512, seq)
    nt = seq // tq
    nb = seq // NSA_SEL_BLOCK
    nc = seq // NSA_CMP_STRIDE
    hpg, g_n, dh = NSA_HPG, NSA_KV_HEADS, HEAD_DIM
    gw = hpg * dh
    q_cols = N_HEADS * dh
    kv_col0 = q_cols // dh
    z_col0 = (q_cols + 3 * 2 * g_n * dh) // gw
    rows = hpg * tq
    sq = pl.Squeezed()

    def kv_spec(branch, kv):
        c = kv_col0 + branch * 2 * g_n + kv * g_n
        return pl.BlockSpec((seq, dh), lambda b, g, i: (b, c + g))

    def z_spec(branch):
        c = z_col0 + branch * g_n
        return pl.BlockSpec((tq, gw), lambda b, g, i: (b * nt + i, c + g))

    win_spec = lambda kv: pl.BlockSpec((sq, seq + NSA_WINDOW, dh), lambda b, g, i: (b, 0, kv * g_n + g))
    cmp_spec = lambda kv: pl.BlockSpec((sq, sq, nc, dh), lambda b, g, i: (kv, b * g_n + g, 0, 0))
    in_specs = [
        pl.BlockSpec(memory_space=pltpu.SMEM),
        pl.BlockSpec((tq, gw), lambda b, g, i: (b * nt + i, g)),
        z_spec(0), z_spec(1), z_spec(2),
        pl.BlockSpec((tq, LANES), lambda b, g, i: (b * nt + i, 0)),
        kv_spec(1, 0), kv_spec(1, 1),
        win_spec(0), win_spec(1),
        cmp_spec(0), cmp_spec(1),
        pl.BlockSpec(e_mat.shape, lambda b, g, i: (0, 0)),
        pl.BlockSpec((nb, nc), lambda b, g, i: (0, 0)),
    ]
    kern = functools.partial(_nsa_kernel, tq=tq, tk=tk, n_top=min(NSA_SEL_TOPN, nb))
    return pl.pallas_call(
        kern,
        grid=(batch, g_n, nt),
        in_specs=in_specs,
        out_specs=pl.BlockSpec((tq, gw), lambda b, g, i: (b * nt + i, g)),
        out_shape=jax.ShapeDtypeStruct((m, q_cols), BF16),
        scratch_shapes=[pltpu.VMEM((tq, seq), F32),
                        pltpu.VMEM((rows, 1), F32),
                        pltpu.VMEM((rows, 1), F32),
                        pltpu.VMEM((rows, dh), F32)],
        compiler_params=pltpu.CompilerParams(
            dimension_semantics=("arbitrary", "arbitrary", "arbitrary"), vmem_limit_bytes=VMEM_LIMIT),
        name="nsa_attention",
    )(slopes, proj, proj, proj, proj, gate_logits, proj, proj, kwp, vwp, kvc, kvc, e_mat, ovl_t)


def _moba_kernel(sl_ref, q_ref, z_ref, k_ref, v_ref, o_ref, km_ref, sel_ref, m_ref, l_ref, acc_ref,
                 *, tq, n_top):
    g = pl.program_id(1)
    i = pl.program_id(2)
    t0 = i * tq
    hpg = MOBA_HPG
    rows = hpg * tq
    seq = k_ref.shape[0]
    nbm = seq // MOBA_BLOCK

    @pl.when(i == 0)
    def _():
        blk = lax.broadcasted_iota(jnp.int32, (nbm, seq), 1) >> (MOBA_BLOCK.bit_length() - 1)
        avg = jnp.where(blk == lax.broadcasted_iota(jnp.int32, (nbm, seq), 0), 1.0 / MOBA_BLOCK, 0.0)
        km_ref[...] = jnp.dot(avg.astype(BF16), k_ref[...], preferred_element_type=F32)

    qa = _stack_heads(q_ref, hpg)
    _, qq = _row_ids(hpg, tq)
    slope = _slope_col(sl_ref, g * hpg, hpg, tq)

    km = km_ref[...]
    km_hi = km.astype(BF16)
    km_lo = (km - km_hi.astype(F32)).astype(BF16)
    sb = (lax.dot_general(km_hi, qa, _NT, preferred_element_type=F32)
          + lax.dot_general(km_lo, qa, _NT, preferred_element_type=F32))
    n_io = lax.broadcasted_iota(jnp.int32, (nbm, rows), 0)
    past = n_io < i
    sb = jnp.where(past, sb, NEG_INF)
    rank = jnp.zeros((nbm, rows), jnp.int32)
    for mm in range(nbm):
        row = sb[mm:mm + 1, :]
        beats = (row > sb) | ((row == sb) & (n_io > mm))
        rank = rank + beats.astype(jnp.int32)
    selneg = jnp.where(past & (rank < n_top), 0.0, NEG_INF).astype(F32)
    if nbm < LANES:
        selneg = jnp.concatenate([selneg, jnp.full((LANES - nbm, rows), NEG_INF, F32)], axis=0)
    sel_ref[...] = selneg.T

    d0 = pl.multiple_of(t0, tq)
    kd = k_ref[pl.ds(d0, tq), :]
    vd = v_ref[pl.ds(d0, tq), :]
    sc = lax.dot_general(qa, kd, _NT, preferred_element_type=F32)
    dd = qq - lax.broadcasted_iota(jnp.int32, (1, tq), 1)
    sc = jnp.where(dd >= 0, sc - slope * dd.astype(F32), NEG_INF)
    mx = jnp.max(sc, axis=-1, keepdims=True)
    pr = jnp.exp(sc - mx)
    m_ref[...] = mx
    l_ref[...] = jnp.sum(pr, axis=-1, keepdims=True)
    acc_ref[...] = jnp.dot(pr.astype(BF16), vd, preferred_element_type=F32)

    lane = lax.broadcasted_iota(jnp.int32, (1, LANES), 1)

    def body(n, carry):
        k0 = pl.multiple_of(n * tq, tq)
        kn = k_ref[pl.ds(k0, tq), :]
        vn = v_ref[pl.ds(k0, tq), :]
        s = lax.dot_general(qa, kn, _NT, preferred_element_type=F32)
        dst = ((t0 + qq) - (k0 + lax.broadcasted_iota(jnp.int32, (1, tq), 1))).astype(F32)
        col = jnp.sum(jnp.where(lane == n, sel_ref[...], 0.0), axis=-1, keepdims=True)
        s = s - slope * dst + col
        m_old = m_ref[...]
        m_new = jnp.maximum(m_old, jnp.max(s, axis=-1, keepdims=True))
        alpha = jnp.exp(m_old - m_new)
        p = jnp.exp(s - m_new)
        l_ref[...] = alpha * l_ref[...] + jnp.sum(p, axis=-1, keepdims=True)
        acc_ref[...] = alpha * acc_ref[...] + jnp.dot(p.astype(BF16), vn, preferred_element_type=F32)
        m_ref[...] = m_new
        return carry

    lax.fori_loop(0, i, body, 0)
    o = acc_ref[...] / l_ref[...]
    for h in range(hpg):
        cs = slice(h * HEAD_DIM, (h + 1) * HEAD_DIM)
        z = z_ref[:, cs].astype(F32)
        o_ref[:, cs] = (o[h * tq:(h + 1) * tq] * (z * _sigmoid(z))).astype(o_ref.dtype)


def _moba_attention(slopes, qz, kv, *, batch, seq):
    m = qz.shape[0]
    tq = MOBA_BLOCK
    nt = seq // tq
    hpg, g_n, dh = MOBA_HPG, MOBA_KV_HEADS, HEAD_DIM
    gw = hpg * dh
    rows = hpg * tq
    nbm = seq // MOBA_BLOCK
    in_specs = [
        pl.BlockSpec(memory_space=pltpu.SMEM),
        pl.BlockSpec((tq, gw), lambda b, g, i: (b * nt + i, g)),
        pl.BlockSpec((tq, gw), lambda b, g, i: (b * nt + i, g_n + g)),
        pl.BlockSpec((seq, dh), lambda b, g, i: (b, g)),
        pl.BlockSpec((seq, dh), lambda b, g, i: (b, g_n + g)),
    ]
    kern = functools.partial(_moba_kernel, tq=tq, n_top=min(MOBA_TOPK, nbm))
    return pl.pallas_call(
        kern,
        grid=(batch, g_n, nt),
        in_specs=in_specs,
        out_specs=pl.BlockSpec((tq, gw), lambda b, g, i: (b * nt + i, g)),
        out_shape=jax.ShapeDtypeStruct((m, N_HEADS * dh), BF16),
        scratch_shapes=[pltpu.VMEM((nbm, dh), F32),
                        pltpu.VMEM((rows, LANES), F32),
                        pltpu.VMEM((rows, 1), F32),
                        pltpu.VMEM((rows, 1), F32),
                        pltpu.VMEM((rows, dh), F32)],
        compiler_params=pltpu.CompilerParams(
            dimension_semantics=("arbitrary", "arbitrary", "arbitrary"), vmem_limit_bytes=VMEM_LIMIT),
        name="moba_attention",
    )(slopes, qz, qz, kv, kv)


def _alibi_slopes():
    return 2.0 ** (-8.0 * jnp.arange(1, N_HEADS + 1, dtype=F32) / N_HEADS)


def _nsa_layer(h, norm_g, w_in, pos_k, pos_v, w1_k, w2_k, w1_v, w2_v, w_out, *, batch, seq):
    dh, g_n = HEAD_DIM, NSA_KV_HEADS
    q_cols = N_HEADS * dh
    kv_cols = 3 * 2 * g_n * dh
    n_main = q_cols + kv_cols + 3 * q_cols
    n_gate = w_in.shape[1] - n_main
    assert n_gate == 3 * N_HEADS <= LANES
    (xn,) = _rmsnorm(h, norm_g[None, :], BF16)
    scale = jnp.where(jnp.arange(n_main) < q_cols, dh ** -0.5, 1.0).astype(F32)[None, :]
    proj = _matmul(xn, w_in[:, :n_main].astype(BF16), tn=min(1024, n_main), out_dtype=BF16, scale=scale,
                   name="nsa_in_proj")
    w_gate = jnp.pad(w_in[:, n_main:], ((0, 0), (0, LANES - n_gate))).astype(BF16)
    gate_logits = _matmul(xn, w_gate, tn=LANES, out_dtype=BF16, scale=jnp.ones((1, LANES), F32),
                          name="nsa_gate_proj")

    half = NSA_CMP_STRIDE
    nr = seq // half
    raw = proj[:, q_cols:q_cols + 2 * g_n * dh].reshape(batch, nr, half, 2, g_n, dh)
    r = raw.transpose(3, 0, 4, 1, 2, 5).reshape(2, batch * g_n, nr, half * dh)
    pos = jnp.stack([pos_k, pos_v])
    pa = pos[:, :half].reshape(2, 1, half * dh)
    pb = pos[:, half:].reshape(2, 1, half * dh)
    w1 = jnp.stack([w1_k, w1_v])
    w1a = w1[:, :half].reshape(2, half * dh, dh).astype(BF16)
    w1b = w1[:, half:].reshape(2, half * dh, dh).astype(BF16)
    w2 = jnp.stack([w2_k, w2_v]).astype(BF16)
    kvc = _compress(r, pa, pb, w1a, w1b, w2)

    wcol = q_cols + 2 * 2 * g_n * dh
    win = proj[:, wcol:wcol + 2 * g_n * dh].reshape(batch, seq, 2 * g_n * dh)
    win = jnp.pad(win, ((0, 0), (NSA_WINDOW, 0), (0, 0)))
    kwp = win
    vwp = win

    nb = seq // NSA_SEL_BLOCK
    e_mat = (jnp.arange(seq)[None, :] // NSA_SEL_BLOCK
             == jnp.arange(max(nb, LANES))[:, None]).astype(BF16)
    cstart = jnp.arange(nr)[None, :] * NSA_CMP_STRIDE
    sstart = jnp.arange(nb)[:, None] * NSA_SEL_BLOCK
    ovl_t = ((cstart <= sstart + NSA_SEL_BLOCK - 1)
             & (cstart + NSA_CMP_LEN - 1 >= sstart)
             & (jnp.arange(nr)[None, :] < (seq - NSA_CMP_LEN) // NSA_CMP_STRIDE + 1)).astype(BF16)
    mix = _nsa_attention(_alibi_slopes(), proj, gate_logits, kwp, vwp, kvc, e_mat, ovl_t,
                         batch=batch, seq=seq)
    return _matmul(mix, w_out.astype(BF16), tn=min(1024, w_out.shape[1]), out_dtype=F32,
                   residual=h, name="nsa_out_proj")


def _moba_shared_kv(xn_kv, kv_w):
    one = jnp.ones((1, kv_w.shape[1]), F32)
    return _matmul(xn_kv, kv_w.astype(BF16), tn=min(1024, kv_w.shape[1]), out_dtype=BF16, scale=one,
                   name="moba_kv_proj")


def _moba_layer(h, xn_q, kv, w_in, w_out, *, batch, seq):
    dh = HEAD_DIM
    q_cols = N_HEADS * dh
    scale = jnp.where(jnp.arange(w_in.shape[1]) < q_cols, dh ** -0.5, 1.0).astype(F32)[None, :]
    qz = _matmul(xn_q, w_in.astype(BF16), tn=min(1024, w_in.shape[1]), out_dtype=BF16, scale=scale,
                 name="moba_in_proj")
    o = _moba_attention(_alibi_slopes(), qz, kv, batch=batch, seq=seq)
    return _matmul(o, w_out.astype(BF16), tn=min(1024, w_out.shape[1]), out_dtype=F32,
                   residual=h, name="moba_out_proj")


def kernel(x, a_norm_g, a_w_in, a_cmp_pos_k, a_cmp_pos_v, a_cmp_w1_k, a_cmp_w2_k, a_cmp_w1_v, a_cmp_w2_v,
           a_w_out, kv_norm_g, kv_w, b_norm_g, b_w_in, b_w_out, final_norm_g):
    batch, seq, d = x.shape
    h = x.reshape(batch * seq, d)
    for layer in range(a_norm_g.shape[0]):
        h = _nsa_layer(h, a_norm_g[layer], a_w_in[layer], a_cmp_pos_k[layer], a_cmp_pos_v[layer],
                       a_cmp_w1_k[layer], a_cmp_w2_k[layer], a_cmp_w1_v[layer], a_cmp_w2_v[layer],
                       a_w_out[layer], batch=batch, seq=seq)
    kv = None
    for layer in range(b_norm_g.shape[0]):
        if layer == 0:
            xn_kv, xn_q = _rmsnorm(h, jnp.stack([kv_norm_g, b_norm_g[layer]]), BF16)
            kv = _moba_shared_kv(xn_kv, kv_w)
        else:
            (xn_q,) = _rmsnorm(h, b_norm_g[layer][None, :], BF16)
        h = _moba_layer(h, xn_q, kv, b_w_in[layer], b_w_out[layer], batch=batch, seq=seq)
    (out,) = _rmsnorm(h, final_norm_g[None, :], F32)
    return out.reshape(batch, seq, d)
```

```python
import functools

import jax
import jax.numpy as jnp
from jax import lax
from jax.experimental import pallas as pl
from jax.experimental.pallas import tpu as pltpu

F32 = jnp.float32
BF16 = jnp.bfloat16

N_HEADS = 32
HEAD_DIM = 128
NSA_KV_HEADS = 4
NSA_HPG = N_HEADS // NSA_KV_HEADS
NSA_CMP_LEN = 32
NSA_CMP_STRIDE = 16
NSA_SEL_BLOCK = 64
NSA_SEL_TOPN = 16
NSA_WINDOW = 512
MOBA_KV_HEADS = 8
MOBA_HPG = N_HEADS // MOBA_KV_HEADS
MOBA_BLOCK = 256
MOBA_TOPK = 3
RMS_EPS = 1e-6
NEG_INF = -1e30
FORCE_SCORE = 1e9
LOG2E = 1.4426950408889634

LANES = 128
VMEM_LIMIT = 56 * 1024 * 1024
POS_COL = 120
POS_SPLIT = 64

_NT = (((1,), (1,)), ((), ()))


def _sigmoid(x):
    return 1.0 / (1.0 + jnp.exp(-x))


def _rmsnorm_kernel(x_ref, g_ref, *o_refs):
    x = x_ref[...].astype(F32)
    y = x * lax.rsqrt(jnp.mean(x * x, axis=-1, keepdims=True) + RMS_EPS)
    for i, o_ref in enumerate(o_refs):
        o_ref[...] = (y * g_ref[i:i + 1, :]).astype(o_ref.dtype)


def _rmsnorm(x, gains, out_dtype):
    m, d = x.shape
    n = gains.shape[0]
    tm = min(256, m)
    outs = pl.pallas_call(
        _rmsnorm_kernel,
        grid=(m // tm,),
        in_specs=[pl.BlockSpec((tm, d), lambda i: (i, 0)),
                  pl.BlockSpec((n, d), lambda i: (0, 0))],
        out_specs=[pl.BlockSpec((tm, d), lambda i: (i, 0)) for _ in range(n)],
        out_shape=[jax.ShapeDtypeStruct((m, d), out_dtype) for _ in range(n)],
        compiler_params=pltpu.CompilerParams(
            dimension_semantics=("arbitrary",), vmem_limit_bytes=VMEM_LIMIT),
        name="rmsnorm",
    )(x, gains)
    return outs


def _mm_scale_kernel(a_ref, w_ref, s_ref, o_ref):
    acc = jnp.dot(a_ref[...], w_ref[...], preferred_element_type=F32)
    o_ref[...] = (acc * s_ref[...]).astype(o_ref.dtype)


def _mm_resid_kernel(a_ref, w_ref, r_ref, o_ref):
    acc = jnp.dot(a_ref[...], w_ref[...], preferred_element_type=F32)
    o_ref[...] = (r_ref[...] + acc).astype(o_ref.dtype)


def _matmul(a, w, *, tn, out_dtype, scale=None, residual=None, col0=0, n_cols=None, name):
    m, k = a.shape
    n = w.shape[1] - col0 if n_cols is None else n_cols
    assert col0 % tn == 0 and n % tn == 0
    cb = col0 // tn
    tm = min(1024, m)
    a_spec = pl.BlockSpec((tm, k), lambda i, j: (i, 0))
    w_spec = pl.BlockSpec((k, tn), lambda i, j: (0, cb + j))
    o_spec = pl.BlockSpec((tm, tn), lambda i, j: (i, j))
    if residual is None:
        kern = _mm_scale_kernel
        extra, extra_spec = scale, pl.BlockSpec((1, tn), lambda i, j: (0, j))
    else:
        kern = _mm_resid_kernel
        extra, extra_spec = residual, o_spec
    return pl.pallas_call(
        kern,
        grid=(m // tm, n // tn),
        in_specs=[a_spec, w_spec, extra_spec],
        out_specs=o_spec,
        out_shape=jax.ShapeDtypeStruct((m, n), out_dtype),
        compiler_params=pltpu.CompilerParams(
            dimension_semantics=("arbitrary", "arbitrary"), vmem_limit_bytes=VMEM_LIMIT),
        name=name,
    )(a, w, extra)


def _cmp_kernel(r_ref, pa_ref, pb_ref, w1a_ref, w1b_ref, w2_ref, o_ref):
    r = r_ref[...].astype(F32)
    nr = r.shape[0]
    xa = (r + pa_ref[...]).astype(BF16)
    xb = (r + pb_ref[...]).astype(BF16)
    ya = jnp.dot(xa, w1a_ref[...], preferred_element_type=F32)
    yb = jnp.dot(xb, w1b_ref[...], preferred_element_type=F32)
    hid = ya + pltpu.roll(yb, nr - 1, 0)
    hid = hid * _sigmoid(hid)
    o_ref[...] = jnp.dot(hid.astype(BF16), w2_ref[...], preferred_element_type=F32).astype(o_ref.dtype)


def _compress(r, pa, pb, w1a, w1b, w2):
    two, bg, nr, kk = r.shape
    dh = w2.shape[-1]
    sq = pl.Squeezed()
    wspec = lambda shape: pl.BlockSpec((sq,) + shape, lambda t, i: (t, 0, 0))
    return pl.pallas_call(
        _cmp_kernel,
        grid=(two, bg),
        in_specs=[pl.BlockSpec((sq, sq, nr, kk), lambda t, i: (t, i, 0, 0)),
                  wspec((1, kk)), wspec((1, kk)), wspec((kk, dh)), wspec((kk, dh)), wspec((dh, dh))],
        out_specs=pl.BlockSpec((sq, sq, nr, dh), lambda t, i: (t, i, 0, 0)),
        out_shape=jax.ShapeDtypeStruct((two, bg, nr, dh), BF16),
        compiler_params=pltpu.CompilerParams(
            dimension_semantics=("arbitrary", "arbitrary"), vmem_limit_bytes=VMEM_LIMIT),
        name="nsa_compress",
    )(r, pa, pb, w1a, w1b, w2)


def _stack_heads(ref, n_heads):
    return jnp.concatenate([ref[:, h * HEAD_DIM:(h + 1) * HEAD_DIM] for h in range(n_heads)], axis=0)


def _query_in_tile(n_heads, tq, width):
    assert tq & (tq - 1) == 0, "query tile must be a power of two"
    return lax.broadcasted_iota(jnp.int32, (n_heads * tq, width), 0) & (tq - 1)


def _slope_extras(sl_ref, head0, n_heads, tq):
    lane = lax.broadcasted_iota(jnp.int32, (tq, LANES), 1)
    blocks = []
    for h in range(n_heads):
        x = jnp.zeros((tq, LANES), F32)
        for piece in range(3):
            hit = (lane == POS_COL + piece) | (lane == POS_COL + 3 + piece)
            x = jnp.where(hit, sl_ref[piece, head0 + h], x)
        blocks.append(x)
    return jnp.concatenate(blocks, axis=0)


def _pad_rows_to_lanes(x):
    n = x.shape[0]
    assert n <= POS_COL, "block one-hot columns must stay clear of the slope/position columns"
    return jnp.concatenate([x, jnp.zeros((LANES - n, x.shape[1]), x.dtype)], axis=0)


def _online_tile(s_ref, p_ref, m_ref, l_ref, a_ref, acc_ref, v, *, rows, width, first):
    nch = width // LANES
    mt = s_ref[:, 0:LANES]
    for c in range(1, nch):
        mt = jnp.maximum(mt, s_ref[:, c * LANES:(c + 1) * LANES])
    mrow = jnp.max(mt, axis=-1, keepdims=True)
    if first:
        m_new = jnp.broadcast_to(mrow, (rows, LANES))
    else:
        m_old = m_ref[...]
        m_new = jnp.maximum(m_old, mrow)
        a_ref[...] = jnp.exp2(m_old - m_new)
    m_ref[...] = m_new
    ps = None
    for c in range(nch):
        cs = slice(c * LANES, (c + 1) * LANES)
        p = jnp.exp2(s_ref[:, cs] - m_new)
        p_ref[:, cs] = p.astype(BF16)
        ps = p if ps is None else ps + p
    if first:
        l_ref[...] = ps
    else:
        l_ref[...] = a_ref[...] * l_ref[...] + ps
    pv = jnp.dot(p_ref[:, 0:width], v, preferred_element_type=F32)
    if first:
        acc_ref[...] = pv
    else:
        acc_ref[...] = a_ref[...] * acc_ref[...] + pv


def _finish(l_ref, acc_ref):
    return acc_ref[...] * (1.0 / jnp.sum(l_ref[...], axis=-1, keepdims=True))


def _reset_stats(m_ref, l_ref, acc_ref):
    m_ref[...] = jnp.full(m_ref.shape, NEG_INF, F32)
    l_ref[...] = jnp.zeros(l_ref.shape, F32)
    acc_ref[...] = jnp.zeros(acc_ref.shape, F32)


def _key_extras(pos, block=None):
    pos = pos[:, None]
    col = jnp.arange(LANES)[None, :]
    x = jnp.zeros((pos.shape[0], LANES), F32)
    if block is not None:
        x = jnp.where(col == pos // block, 1.0, x)
    x = jnp.where((col >= POS_COL) & (col < POS_COL + 3), (pos // POS_SPLIT) * POS_SPLIT, x)
    x = jnp.where((col >= POS_COL + 3) & (col < POS_COL + 6), pos % POS_SPLIT, x)
    return x.astype(BF16)


def _slope_pieces():
    s = 2.0 ** (-8.0 * jnp.arange(1, N_HEADS + 1, dtype=F32) / N_HEADS) * LOG2E
    a = s.astype(BF16).astype(F32)
    b = (s - a).astype(BF16).astype(F32)
    c = (s - a - b).astype(BF16).astype(F32)
    return jnp.stack([a, b, c])


def _nsa_kernel(sl_ref, q_ref, zc_ref, zs_ref, zw_ref, gt_ref, ks_ref, vs_ref, kw_ref, vw_ref,
                kc_ref, vc_ref, kx_ref, wx_ref, cx_ref, ovl_ref, o_ref,
                s_ref, p_ref, m_ref, l_ref, a_ref, acc_ref, ob_ref, *, tq, tk, n_top):
    g = pl.program_id(1)
    i = pl.program_id(2)
    t0 = i * tq
    hpg = NSA_HPG
    rows = hpg * tq
    qa = _stack_heads(q_ref, hpg)
    qx0 = _slope_extras(sl_ref, g * hpg, hpg, tq)
    q_aug0 = jnp.concatenate([qa, qx0.astype(BF16)], axis=1)
    qq = _query_in_tile(hpg, tq, LANES)
    kk = lax.broadcasted_iota(jnp.int32, (rows, LANES), 1)
    lower_add = jnp.where(kk <= qq, 0.0, NEG_INF)
    upper_add = jnp.where(kk > qq, 0.0, NEG_INF)

    nc = kc_ref.shape[0]
    kc_aug = jnp.concatenate([kc_ref[...], cx_ref[...]], axis=1)
    s = lax.dot_general(q_aug0, kc_aug, _NT, preferred_element_type=F32)
    cend = lax.broadcasted_iota(jnp.int32, (1, nc), 1) * NSA_CMP_STRIDE + (NSA_CMP_LEN - 1)
    tqv = t0 + (lax.broadcasted_iota(jnp.int32, (rows, 1), 0) & (tq - 1))
    mask = cend <= tqv
    s = jnp.where(mask, s, NEG_INF)
    mx = jnp.max(s, axis=-1, keepdims=True)
    e = jnp.where(mask, jnp.exp2(s - mx), 0.0)
    den = jnp.sum(e, axis=-1, keepdims=True)
    p = e * (1.0 / jnp.where(den > 0, den, 1.0))
    ob_ref[0] = jnp.dot(p.astype(BF16), vc_ref[...], preferred_element_type=F32)
    psum = p[0:tq]
    for h in range(1, hpg):
        psum = psum + p[h * tq:(h + 1) * tq]

    p_hi = psum.astype(BF16)
    p_lo = (psum - p_hi.astype(F32)).astype(BF16)
    ovl = ovl_ref[...]
    imp = (lax.dot_general(ovl, p_hi, _NT, preferred_element_type=F32)
           + lax.dot_general(ovl, p_lo, _NT, preferred_element_type=F32))
    nb = imp.shape[0]
    j = lax.broadcasted_iota(jnp.int32, (nb, tq), 0)
    blkq = (t0 + lax.broadcasted_iota(jnp.int32, (nb, tq), 1)) >> (NSA_SEL_BLOCK.bit_length() - 1)
    forced = (j == 0) | (j == blkq) | (j == blkq - 1)
    imp = jnp.where(forced, FORCE_SCORE, jnp.where(j > blkq, NEG_INF, imp))
    rank = jnp.zeros((nb, tq), jnp.int32)
    for ii in range(nb):
        row = imp[ii:ii + 1, :]
        beats = (row > imp) | ((row == imp) & (j > ii))
        rank = rank + beats.astype(jnp.int32)
    selneg = jnp.where((rank < n_top) & (j <= blkq), 0.0, NEG_INF).astype(F32)
    selneg_q = _pad_rows_to_lanes(selneg).T
    qx = qx0 + jnp.concatenate([selneg_q] * hpg, axis=0)
    q_aug = jnp.concatenate([qa, qx.astype(BF16)], axis=1)

    wlen = NSA_WINDOW + tq
    w0 = pl.multiple_of(t0, tq)
    kw_aug = jnp.concatenate([kw_ref[pl.ds(w0, wlen), :], wx_ref[pl.ds(w0, wlen), :]], axis=1)
    s_ref[:, 0:wlen] = lax.dot_general(q_aug0, kw_aug, _NT, preferred_element_type=F32)
    s_ref[:, 0:LANES] = s_ref[:, 0:LANES] + upper_add
    s_ref[:, wlen - LANES:wlen] = s_ref[:, wlen - LANES:wlen] + lower_add
    _online_tile(s_ref, p_ref, m_ref, l_ref, a_ref, acc_ref, vw_ref[pl.ds(w0, wlen), :],
                 rows=rows, width=wlen, first=True)
    ob_ref[2] = _finish(l_ref, acc_ref)

    _reset_stats(m_ref, l_ref, acc_ref)
    nkt = (t0 + tq + tk - 1) // tk

    def sel_tile(kt, last):
        k0 = pl.multiple_of(kt * tk, tk)
        k_aug = jnp.concatenate([ks_ref[pl.ds(k0, tk), :], kx_ref[pl.ds(k0, tk), :]], axis=1)
        s_ref[:, 0:tk] = lax.dot_general(q_aug, k_aug, _NT, preferred_element_type=F32)
        if last:
            off = pl.multiple_of(t0 - k0, LANES)
            s_ref[:, pl.ds(off, LANES)] = s_ref[:, pl.ds(off, LANES)] + lower_add
        _online_tile(s_ref, p_ref, m_ref, l_ref, a_ref, acc_ref, vs_ref[pl.ds(k0, tk), :],
                     rows=rows, width=tk, first=False)

    def sel_body(kt, carry):
        sel_tile(kt, False)
        return carry

    lax.fori_loop(0, nkt - 1, sel_body, 0)
    sel_tile(nkt - 1, True)
    ob_ref[1] = _finish(l_ref, acc_ref)

    gts = gt_ref[...].astype(F32)
    gts = pltpu.roll(gts, (LANES - g * hpg) % LANES, 1)
    gts = _sigmoid(gts)
    for h in range(hpg):
        rs = slice(h * tq, (h + 1) * tq)
        cs = slice(h * HEAD_DIM, (h + 1) * HEAD_DIM)
        mix = jnp.zeros((tq, HEAD_DIM), F32)
        for br, z_ref in enumerate((zc_ref, zs_ref, zw_ref)):
            z = z_ref[:, cs].astype(F32)
            gate = gts[:, br * N_HEADS + h:br * N_HEADS + h + 1]
            mix = mix + gate * ob_ref[br, rs, :] * (z * _sigmoid(z))
        o_ref[:, cs] = mix.astype(o_ref.dtype)


def _nsa_attention(pieces, proj, gate_logits, win, kvc, ovl_t, *, batch, seq):
    m = proj.shape[0]
    tq = min(128, seq)
    tk = min(512, seq)
    nt = seq // tq
    nb = seq // NSA_SEL_BLOCK
    nc = seq // NSA_CMP_STRIDE
    hpg, g_n, dh = NSA_HPG, NSA_KV_HEADS, HEAD_DIM
    gw = hpg * dh
    q_cols = N_HEADS * dh
    kv_col0 = q_cols // dh
    z_col0 = (q_cols + 3 * 2 * g_n * dh) // gw
    rows = hpg * tq
    wlen = NSA_WINDOW + tq
    sq = pl.Squeezed()
    assert nb <= POS_COL
    kx = _key_extras(jnp.arange(seq), NSA_SEL_BLOCK)
    wx = _key_extras(jnp.arange(seq + NSA_WINDOW))
    cx = _key_extras(jnp.arange(nc) * NSA_CMP_STRIDE + (NSA_CMP_LEN - 1))

    def kv_spec(branch, kv):
        c = kv_col0 + branch * 2 * g_n + kv * g_n
        return pl.BlockSpec((seq, dh), lambda b, g, i: (b, c + g))

    def z_spec(branch):
        c = z_col0 + branch * g_n
        return pl.BlockSpec((tq, gw), lambda b, g, i: (b * nt + i, c + g))

    win_spec = lambda kv: pl.BlockSpec((sq, seq + NSA_WINDOW, dh), lambda b, g, i: (b, 0, kv * g_n + g))
    cmp_spec = lambda kv: pl.BlockSpec((sq, sq, nc, dh), lambda b, g, i: (kv, b * g_n + g, 0, 0))
    full = lambda arr: pl.BlockSpec(arr.shape, lambda b, g, i: (0,) * arr.ndim)
    in_specs = [
        pl.BlockSpec(memory_space=pltpu.SMEM),
        pl.BlockSpec((tq, gw), lambda b, g, i: (b * nt + i, g)),
        z_spec(0), z_spec(1), z_spec(2),
        pl.BlockSpec((tq, LANES), lambda b, g, i: (b * nt + i, 0)),
        kv_spec(1, 0), kv_spec(1, 1),
        win_spec(0), win_spec(1),
        cmp_spec(0), cmp_spec(1),
        full(kx), full(wx), full(cx), full(ovl_t),
    ]
    kern = functools.partial(_nsa_kernel, tq=tq, tk=tk, n_top=min(NSA_SEL_TOPN, nb))
    return pl.pallas_call(
        kern,
        grid=(batch, g_n, nt),
        in_specs=in_specs,
        out_specs=pl.BlockSpec((tq, gw), lambda b, g, i: (b * nt + i, g)),
        out_shape=jax.ShapeDtypeStruct((m, q_cols), BF16),
        scratch_shapes=[pltpu.VMEM((rows, wlen), F32),
                        pltpu.VMEM((rows, wlen), BF16),
                        pltpu.VMEM((rows, LANES), F32),
                        pltpu.VMEM((rows, LANES), F32),
                        pltpu.VMEM((rows, LANES), F32),
                        pltpu.VMEM((rows, dh), F32),
                        pltpu.VMEM((3, rows, dh), F32)],
        compiler_params=pltpu.CompilerParams(
            dimension_semantics=("arbitrary", "arbitrary", "arbitrary"), vmem_limit_bytes=VMEM_LIMIT),
        name="nsa_attention",
    )(pieces, proj, proj, proj, proj, gate_logits, proj, proj, win, win, kvc, kvc, kx, wx, cx, ovl_t)


def _moba_kernel(sl_ref, q_ref, z_ref, k_ref, v_ref, mx_ref, o_ref,
                 km_ref, s_ref, p_ref, m_ref, l_ref, a_ref, acc_ref, *, tq, tk, n_top):
    g = pl.program_id(1)
    i = pl.program_id(2)
    hpg = MOBA_HPG
    rows = hpg * tq
    seq = k_ref.shape[0]
    nbm = seq // MOBA_BLOCK
    bpt = tk // MOBA_BLOCK

    @pl.when(i == 0)
    def _():
        blk = lax.broadcasted_iota(jnp.int32, (nbm, seq), 1) >> (MOBA_BLOCK.bit_length() - 1)
        avg = jnp.where(blk == lax.broadcasted_iota(jnp.int32, (nbm, seq), 0), 1.0 / MOBA_BLOCK, 0.0)
        km_ref[...] = jnp.dot(avg.astype(BF16), k_ref[...], preferred_element_type=F32)

    qa = _stack_heads(q_ref, hpg)
    qx0 = _slope_extras(sl_ref, g * hpg, hpg, tq)

    km = km_ref[...]
    km_hi = km.astype(BF16)
    km_lo = (km - km_hi.astype(F32)).astype(BF16)
    sb = (lax.dot_general(km_hi, qa, _NT, preferred_element_type=F32)
          + lax.dot_general(km_lo, qa, _NT, preferred_element_type=F32))
    n_io = lax.broadcasted_iota(jnp.int32, (nbm, rows), 0)
    past = n_io < i
    sb = jnp.where(past, sb, NEG_INF)
    rank = jnp.zeros((nbm, rows), jnp.int32)
    for mm in range(nbm):
        row = sb[mm:mm + 1, :]
        beats = (row > sb) | ((row == sb) & (n_io > mm))
        rank = rank + beats.astype(jnp.int32)
    visible = (past & (rank < n_top)) | (n_io == i)
    selneg = jnp.where(visible, 0.0, NEG_INF).astype(F32)
    qx = qx0 + _pad_rows_to_lanes(selneg).T
    q_aug = jnp.concatenate([qa, qx.astype(BF16)], axis=1)

    _reset_stats(m_ref, l_ref, acc_ref)
    ntile = (i + bpt) // bpt

    def tile(t, last):
        k0 = pl.multiple_of(t * tk, tk)
        k_aug = jnp.concatenate([k_ref[pl.ds(k0, tk), :], mx_ref[pl.ds(k0, tk), :]], axis=1)
        s_ref[...] = lax.dot_general(q_aug, k_aug, _NT, preferred_element_type=F32)
        if last:
            qq = _query_in_tile(hpg, tq, LANES)
            kk = lax.broadcasted_iota(jnp.int32, (rows, LANES), 1)
            off = pl.multiple_of(i * MOBA_BLOCK - k0, LANES)
            for c in range(MOBA_BLOCK // LANES):
                tri = jnp.where(kk + c * LANES <= qq, 0.0, NEG_INF)
                sl = pl.ds(off + c * LANES, LANES)
                s_ref[:, sl] = s_ref[:, sl] + tri
        _online_tile(s_ref, p_ref, m_ref, l_ref, a_ref, acc_ref, v_ref[pl.ds(k0, tk), :],
                     rows=rows, width=tk, first=False)

    def body(t, carry):
        tile(t, False)
        return carry

    lax.fori_loop(0, ntile - 1, body, 0)
    tile(ntile - 1, True)
    o = _finish(l_ref, acc_ref)
    for h in range(hpg):
        cs = slice(h * HEAD_DIM, (h + 1) * HEAD_DIM)
        z = z_ref[:, cs].astype(F32)
        o_ref[:, cs] = (o[h * tq:(h + 1) * tq] * (z * _sigmoid(z))).astype(o_ref.dtype)


def _moba_attention(pieces, qz, kv, *, batch, seq):
    m = qz.shape[0]
    tq = MOBA_BLOCK
    tk = min(2 * MOBA_BLOCK, seq)
    nt = seq // tq
    hpg, g_n, dh = MOBA_HPG, MOBA_KV_HEADS, HEAD_DIM
    gw = hpg * dh
    rows = hpg * tq
    nbm = seq // MOBA_BLOCK
    assert nbm <= POS_COL
    mx = _key_extras(jnp.arange(seq), MOBA_BLOCK)
    in_specs = [
        pl.BlockSpec(memory_space=pltpu.SMEM),
        pl.BlockSpec((tq, gw), lambda b, g, i: (b * nt + i, g)),
        pl.BlockSpec((tq, gw), lambda b, g, i: (b * nt + i, g_n + g)),
        pl.BlockSpec((seq, dh), lambda b, g, i: (b, g)),
        pl.BlockSpec((seq, dh), lambda b, g, i: (b, g_n + g)),
        pl.BlockSpec(mx.shape, lambda b, g, i: (0, 0)),
    ]
    kern = functools.partial(_moba_kernel, tq=tq, tk=tk, n_top=min(MOBA_TOPK, nbm))
    return pl.pallas_call(
        kern,
        grid=(batch, g_n, nt),
        in_specs=in_specs,
        out_specs=pl.BlockSpec((tq, gw), lambda b, g, i: (b * nt + i, g)),
        out_shape=jax.ShapeDtypeStruct((m, N_HEADS * dh), BF16),
        scratch_shapes=[pltpu.VMEM((nbm, dh), F32),
                        pltpu.VMEM((rows, tk), F32),
                        pltpu.VMEM((rows, tk), BF16),
                        pltpu.VMEM((rows, LANES), F32),
                        pltpu.VMEM((rows, LANES), F32),
                        pltpu.VMEM((rows, LANES), F32),
                        pltpu.VMEM((rows, dh), F32)],
        compiler_params=pltpu.CompilerParams(
            dimension_semantics=("arbitrary", "arbitrary", "arbitrary"), vmem_limit_bytes=VMEM_LIMIT),
        name="moba_attention",
    )(pieces, qz, qz, kv, kv, mx)


def _nsa_layer(h, norm_g, w_in, pos_k, pos_v, w1_k, w2_k, w1_v, w2_v, w_out, *, batch, seq):
    dh, g_n = HEAD_DIM, NSA_KV_HEADS
    q_cols = N_HEADS * dh
    kv_cols = 3 * 2 * g_n * dh
    n_main = q_cols + kv_cols + 3 * q_cols
    n_gate = w_in.shape[1] - n_main
    assert n_gate == 3 * N_HEADS <= LANES and n_main % LANES == 0
    (xn,) = _rmsnorm(h, norm_g[None, :], BF16)
    w_all = jnp.pad(w_in.astype(BF16), ((0, 0), (0, LANES - n_gate)))
    scale = jnp.where(jnp.arange(n_main) < q_cols, dh ** -0.5 * LOG2E, 1.0).astype(F32)[None, :]
    proj = _matmul(xn, w_all, tn=min(1024, n_main), out_dtype=BF16, scale=scale, n_cols=n_main,
                   name="nsa_in_proj")
    gate_logits = _matmul(xn, w_all, tn=LANES, out_dtype=BF16, scale=jnp.ones((1, LANES), F32),
                          col0=n_main, n_cols=LANES, name="nsa_gate_proj")

    half = NSA_CMP_STRIDE
    nr = seq // half
    raw = proj[:, q_cols:q_cols + 2 * g_n * dh].reshape(batch, nr, half, 2, g_n, dh)
    r = raw.transpose(3, 0, 4, 1, 2, 5).reshape(2, batch * g_n, nr, half * dh)
    pos = jnp.stack([pos_k, pos_v])
    pa = pos[:, :half].reshape(2, 1, half * dh)
    pb = pos[:, half:].reshape(2, 1, half * dh)
    w1 = jnp.stack([w1_k, w1_v])
    w1a = w1[:, :half].reshape(2, half * dh, dh).astype(BF16)
    w1b = w1[:, half:].reshape(2, half * dh, dh).astype(BF16)
    w2 = jnp.stack([w2_k, w2_v]).astype(BF16)
    kvc = _compress(r, pa, pb, w1a, w1b, w2)

    wcol = q_cols + 2 * 2 * g_n * dh
    win = proj[:, wcol:wcol + 2 * g_n * dh].reshape(batch, seq, 2 * g_n * dh)
    win = jnp.pad(win, ((0, 0), (NSA_WINDOW, 0), (0, 0)))

    nb = seq // NSA_SEL_BLOCK
    cstart = jnp.arange(nr)[None, :] * NSA_CMP_STRIDE
    sstart = jnp.arange(nb)[:, None] * NSA_SEL_BLOCK
    ovl_t = ((cstart <= sstart + NSA_SEL_BLOCK - 1)
             & (cstart + NSA_CMP_LEN - 1 >= sstart)
             & (jnp.arange(nr)[None, :] < (seq - NSA_CMP_LEN) // NSA_CMP_STRIDE + 1)).astype(BF16)
    mix = _nsa_attention(_slope_pieces(), proj, gate_logits, win, kvc, ovl_t, batch=batch, seq=seq)
    return _matmul(mix, w_out.astype(BF16), tn=min(1024, w_out.shape[1]), out_dtype=F32,
                   residual=h, name="nsa_out_proj")


def _moba_shared_kv(xn_kv, kv_w):
    one = jnp.ones((1, kv_w.shape[1]), F32)
    return _matmul(xn_kv, kv_w.astype(BF16), tn=min(1024, kv_w.shape[1]), out_dtype=BF16, scale=one,
                   name="moba_kv_proj")


def _moba_layer(h, xn_q, kv, w_in, w_out, *, batch, seq):
    dh = HEAD_DIM
    q_cols = N_HEADS * dh
    scale = jnp.where(jnp.arange(w_in.shape[1]) < q_cols, dh ** -0.5 * LOG2E, 1.0).astype(F32)[None, :]
    qz = _matmul(xn_q, w_in.astype(BF16), tn=min(1024, w_in.shape[1]), out_dtype=BF16, scale=scale,
                 name="moba_in_proj")
    o = _moba_attention(_slope_pieces(), qz, kv, batch=batch, seq=seq)
    return _matmul(o, w_out.astype(BF16), tn=min(1024, w_out.shape[1]), out_dtype=F32,
                   residual=h, name="moba_out_proj")


def kernel(x, a_norm_g, a_w_in, a_cmp_pos_k, a_cmp_pos_v, a_cmp_w1_k, a_cmp_w2_k, a_cmp_w1_v, a_cmp_w2_v,
           a_w_out, kv_norm_g, kv_w, b_norm_g, b_w_in, b_w_out, final_norm_g):
    batch, seq, d = x.shape
    h = x.reshape(batch * seq, d)
    for layer in range(a_norm_g.shape[0]):
        h = _nsa_layer(h, a_norm_g[layer], a_w_in[layer], a_cmp_pos_k[layer], a_cmp_pos_v[layer],
                       a_cmp_w1_k[layer], a_cmp_w2_k[layer], a_cmp_w1_v[layer], a_cmp_w2_v[layer],
                       a_w_out[layer], batch=batch, seq=seq)
    kv = None
    for layer in range(b_norm_g.shape[0]):
        if layer == 0:
            xn_kv, xn_q = _rmsnorm(h, jnp.stack([kv_norm_g, b_norm_g[layer]]), BF16)
            kv = _moba_shared_kv(xn_kv, kv_w)
        else:
            (xn_q,) = _rmsnorm(h, b_norm_g[layer][None, :], BF16)
        h = _moba_layer(h, xn_q, kv, b_w_in[layer], b_w_out[layer], batch=batch, seq=seq)
    (out,) = _rmsnorm(h, final_norm_g[None, :], F32)
    return out.reshape(batch, seq, d)
```

```python
import functools

import jax
import jax.numpy as jnp
from jax import lax
from jax.experimental import pallas as pl
from jax.experimental.pallas import tpu as pltpu

F32 = jnp.float32
BF16 = jnp.bfloat16

N_HEADS = 32
HEAD_DIM = 128
NSA_KV_HEADS = 4
NSA_HPG = N_HEADS // NSA_KV_HEADS
NSA_CMP_LEN = 32
NSA_CMP_STRIDE = 16
NSA_SEL_BLOCK = 64
NSA_SEL_TOPN = 16
NSA_WINDOW = 512
MOBA_KV_HEADS = 8
MOBA_HPG = N_HEADS // MOBA_KV_HEADS
MOBA_BLOCK = 256
MOBA_TOPK = 3
RMS_EPS = 1e-6
NEG_INF = -1e30
FORCE_SCORE = 1e9
LOG2E = 1.4426950408889634

LANES = 128
VMEM_LIMIT = 56 * 1024 * 1024
POS_COL = 120
POS_SPLIT = 64

_NT = (((1,), (1,)), ((), ()))


def _sigmoid(x):
    return 1.0 / (1.0 + jnp.exp(-x))


def _rmsnorm_kernel(x_ref, g_ref, *o_refs):
    x = x_ref[...].astype(F32)
    y = x * lax.rsqrt(jnp.mean(x * x, axis=-1, keepdims=True) + RMS_EPS)
    for i, o_ref in enumerate(o_refs):
        o_ref[...] = (y * g_ref[i:i + 1, :]).astype(o_ref.dtype)


def _rmsnorm(x, gains, out_dtype):
    m, d = x.shape
    n = gains.shape[0]
    tm = min(256, m)
    outs = pl.pallas_call(
        _rmsnorm_kernel,
        grid=(m // tm,),
        in_specs=[pl.BlockSpec((tm, d), lambda i: (i, 0)),
                  pl.BlockSpec((n, d), lambda i: (0, 0))],
        out_specs=[pl.BlockSpec((tm, d), lambda i: (i, 0)) for _ in range(n)],
        out_shape=[jax.ShapeDtypeStruct((m, d), out_dtype) for _ in range(n)],
        compiler_params=pltpu.CompilerParams(
            dimension_semantics=("arbitrary",), vmem_limit_bytes=VMEM_LIMIT),
        name="rmsnorm",
    )(x, gains)
    return outs


def _mm_kernel(a_ref, w_ref, x_ref, o_ref, *, w_is_nk, residual):
    w = w_ref[...].astype(BF16)
    if w_is_nk:
        acc = lax.dot_general(a_ref[...], w, _NT, preferred_element_type=F32)
    else:
        acc = jnp.dot(a_ref[...], w, preferred_element_type=F32)
    o_ref[...] = ((x_ref[...] + acc) if residual else (acc * x_ref[...])).astype(o_ref.dtype)


def _matmul(a, w, *, out_dtype, scale=None, residual=None, n_cols=None, w_is_nk=False, name):
    m, k = a.shape
    n = (w.shape[0] if w_is_nk else w.shape[1]) if n_cols is None else n_cols
    tn = min(512, n)
    assert n % tn == 0
    tm = min(1024, m)
    a_spec = pl.BlockSpec((tm, k), lambda i, j: (i, 0))
    w_spec = pl.BlockSpec((tn, k), lambda i, j: (j, 0)) if w_is_nk else pl.BlockSpec((k, tn), lambda i, j: (0, j))
    o_spec = pl.BlockSpec((tm, tn), lambda i, j: (i, j))
    kern = functools.partial(_mm_kernel, w_is_nk=w_is_nk, residual=residual is not None)
    if residual is None:
        extra, extra_spec = scale, pl.BlockSpec((1, tn), lambda i, j: (0, j))
    else:
        extra, extra_spec = residual, o_spec
    return pl.pallas_call(
        kern,
        grid=(m // tm, n // tn),
        in_specs=[a_spec, w_spec, extra_spec],
        out_specs=o_spec,
        out_shape=jax.ShapeDtypeStruct((m, n), out_dtype),
        compiler_params=pltpu.CompilerParams(
            dimension_semantics=("arbitrary", "arbitrary"), vmem_limit_bytes=VMEM_LIMIT),
        name=name,
    )(a, w, extra)


def _cmp_kernel(r_ref, pa_ref, pb_ref, w1a_ref, w1b_ref, w2_ref, o_ref):
    r = r_ref[...].astype(F32)
    nr = r.shape[0]
    xa = (r + pa_ref[...]).astype(BF16)
    xb = (r + pb_ref[...]).astype(BF16)
    ya = jnp.dot(xa, w1a_ref[...], preferred_element_type=F32)
    yb = jnp.dot(xb, w1b_ref[...], preferred_element_type=F32)
    hid = ya + pltpu.roll(yb, nr - 1, 0)
    hid = hid * _sigmoid(hid)
    o_ref[...] = jnp.dot(hid.astype(BF16), w2_ref[...], preferred_element_type=F32).astype(o_ref.dtype)


def _compress(r, pa, pb, w1a, w1b, w2):
    two, bg, nr, kk = r.shape
    dh = w2.shape[-1]
    sq = pl.Squeezed()
    wspec = lambda shape: pl.BlockSpec((sq,) + shape, lambda t, i: (t, 0, 0))
    return pl.pallas_call(
        _cmp_kernel,
        grid=(two, bg),
        in_specs=[pl.BlockSpec((sq, sq, nr, kk), lambda t, i: (t, i, 0, 0)),
                  wspec((1, kk)), wspec((1, kk)), wspec((kk, dh)), wspec((kk, dh)), wspec((dh, dh))],
        out_specs=pl.BlockSpec((sq, sq, nr, dh), lambda t, i: (t, i, 0, 0)),
        out_shape=jax.ShapeDtypeStruct((two, bg, nr, dh), BF16),
        compiler_params=pltpu.CompilerParams(
            dimension_semantics=("arbitrary", "arbitrary"), vmem_limit_bytes=VMEM_LIMIT),
        name="nsa_compress",
    )(r, pa, pb, w1a, w1b, w2)


def _stack_heads(ref, n_heads):
    return jnp.concatenate([ref[:, h * HEAD_DIM:(h + 1) * HEAD_DIM] for h in range(n_heads)], axis=0)


def _query_in_tile(n_heads, tq, width):
    assert tq & (tq - 1) == 0, "query tile must be a power of two"
    return lax.broadcasted_iota(jnp.int32, (n_heads * tq, width), 0) & (tq - 1)


def _slope_extras(sl_ref, head0, n_heads, tq):
    lane = lax.broadcasted_iota(jnp.int32, (tq, LANES), 1)
    blocks = []
    for h in range(n_heads):
        x = jnp.zeros((tq, LANES), F32)
        for piece in range(3):
            hit = (lane == POS_COL + piece) | (lane == POS_COL + 3 + piece)
            x = jnp.where(hit, sl_ref[piece, head0 + h], x)
        blocks.append(x)
    return jnp.concatenate(blocks, axis=0)


def _causal_add(n_heads, tq, chunk, strict_future):
    qq = _query_in_tile(n_heads, tq, LANES)
    kk = lax.broadcasted_iota(jnp.int32, (n_heads * tq, LANES), 1) + chunk * LANES
    keep = (kk > qq) if strict_future else (kk <= qq)
    return jnp.where(keep, 0.0, NEG_INF)


def _positions_only(kx_tile):
    lane = lax.broadcasted_iota(jnp.int32, kx_tile.shape, 1)
    return jnp.where(lane >= POS_COL, kx_tile, jnp.zeros_like(kx_tile))


def _pad_rows_to_lanes(x):
    n = x.shape[0]
    assert n <= POS_COL, "block one-hot columns must stay clear of the slope/position columns"
    return jnp.concatenate([x, jnp.zeros((LANES - n, x.shape[1]), x.dtype)], axis=0)


def _online_tile(s_ref, p_ref, m_ref, l_ref, a_ref, acc_ref, v, *, rows, width, first):
    nch = width // LANES
    mt = s_ref[:, 0:LANES]
    for c in range(1, nch):
        mt = jnp.maximum(mt, s_ref[:, c * LANES:(c + 1) * LANES])
    mrow = jnp.max(mt, axis=-1, keepdims=True)
    if first:
        m_new = jnp.broadcast_to(mrow, (rows, LANES))
    else:
        m_old = m_ref[...]
        m_new = jnp.maximum(m_old, mrow)
        a_ref[...] = jnp.exp2(m_old - m_new)
    m_ref[...] = m_new
    ps = None
    for c in range(nch):
        cs = slice(c * LANES, (c + 1) * LANES)
        p = jnp.exp2(s_ref[:, cs] - m_new)
        p_ref[:, cs] = p.astype(BF16)
        ps = p if ps is None else ps + p
    if first:
        l_ref[...] = ps
    else:
        l_ref[...] = a_ref[...] * l_ref[...] + ps
    pv = jnp.dot(p_ref[:, 0:width], v, preferred_element_type=F32)
    if first:
        acc_ref[...] = pv
    else:
        acc_ref[...] = a_ref[...] * acc_ref[...] + pv


def _finish(l_ref, acc_ref):
    return acc_ref[...] * (1.0 / jnp.sum(l_ref[...], axis=-1, keepdims=True))


def _key_extras(pos, block=None):
    pos = pos[:, None]
    col = jnp.arange(LANES)[None, :]
    x = jnp.zeros((pos.shape[0], LANES), F32)
    if block is not None:
        x = jnp.where(col == pos // block, 1.0, x)
    x = jnp.where((col >= POS_COL) & (col < POS_COL + 3), (pos // POS_SPLIT) * POS_SPLIT, x)
    x = jnp.where((col >= POS_COL + 3) & (col < POS_COL + 6), pos % POS_SPLIT, x)
    return x.astype(BF16)


def _slope_pieces():
    s = 2.0 ** (-8.0 * jnp.arange(1, N_HEADS + 1, dtype=F32) / N_HEADS) * LOG2E
    a = s.astype(BF16).astype(F32)
    b = (s - a).astype(BF16).astype(F32)
    c = (s - a - b).astype(BF16).astype(F32)
    return jnp.stack([a, b, c])


def _nsa_kernel(sl_ref, q_ref, zc_ref, zs_ref, zw_ref, gt_ref, ks_ref, vs_ref, kw_ref, vw_ref,
                kc_ref, vc_ref, kx_ref, wx_ref, cx_ref, ovl_ref, o_ref,
                s_ref, p_ref, m_ref, l_ref, a_ref, acc_ref, ob_ref, *, tq, tk, n_top):
    g = pl.program_id(1)
    i = pl.program_id(2)
    t0 = i * tq
    hpg = NSA_HPG
    rows = hpg * tq
    qa = _stack_heads(q_ref, hpg)
    qx0 = _slope_extras(sl_ref, g * hpg, hpg, tq).astype(BF16)
    q_aug0 = jnp.concatenate([qa, qx0], axis=1)
    nqc = tq // LANES

    nc = kc_ref.shape[0]
    kc_aug = jnp.concatenate([kc_ref[...], cx_ref[...]], axis=1)
    s = lax.dot_general(q_aug0, kc_aug, _NT, preferred_element_type=F32)
    cend = lax.broadcasted_iota(jnp.int32, (1, nc), 1) * NSA_CMP_STRIDE + (NSA_CMP_LEN - 1)
    tqv = t0 + (lax.broadcasted_iota(jnp.int32, (rows, 1), 0) & (tq - 1))
    mask = cend <= tqv
    s = jnp.where(mask, s, NEG_INF)
    mx = jnp.max(s, axis=-1, keepdims=True)
    e = jnp.where(mask, jnp.exp2(s - mx), 0.0)
    den = jnp.sum(e, axis=-1, keepdims=True)
    p = e * (1.0 / jnp.where(den > 0, den, 1.0))
    ob_ref[0] = jnp.dot(p.astype(BF16), vc_ref[...], preferred_element_type=F32)
    psum = p[0:tq]
    for h in range(1, hpg):
        psum = psum + p[h * tq:(h + 1) * tq]

    p_hi = psum.astype(BF16)
    p_lo = (psum - p_hi.astype(F32)).astype(BF16)
    ovl = ovl_ref[...]
    imp = (lax.dot_general(ovl, p_hi, _NT, preferred_element_type=F32)
           + lax.dot_general(ovl, p_lo, _NT, preferred_element_type=F32))
    nb = imp.shape[0]
    j = lax.broadcasted_iota(jnp.int32, (nb, tq), 0)
    blkq = (t0 + lax.broadcasted_iota(jnp.int32, (nb, tq), 1)) >> (NSA_SEL_BLOCK.bit_length() - 1)
    forced = (j == 0) | (j == blkq) | (j == blkq - 1)
    imp = jnp.where(forced, FORCE_SCORE, jnp.where(j > blkq, NEG_INF, imp))
    rank = jnp.zeros((nb, tq), jnp.int32)
    for ii in range(nb):
        row = imp[ii:ii + 1, :]
        beats = (row > imp) | ((row == imp) & (j > ii))
        rank = rank + beats.astype(jnp.int32)
    blk0 = t0 >> (NSA_SEL_BLOCK.bit_length() - 1)
    selneg = jnp.where((rank < n_top) & (j < blk0), 0.0, NEG_INF).astype(F32)
    selneg_q = _pad_rows_to_lanes(selneg).T.astype(BF16)
    qx = qx0 + jnp.concatenate([selneg_q] * hpg, axis=0)
    q_aug = jnp.concatenate([qa, qx], axis=1)
    d0 = pl.multiple_of(t0, tq)

    wlen = NSA_WINDOW + tq
    kw_aug = jnp.concatenate([kw_ref[pl.ds(d0, wlen), :], wx_ref[pl.ds(d0, wlen), :]], axis=1)
    s_ref[:, 0:wlen] = lax.dot_general(q_aug0, kw_aug, _NT, preferred_element_type=F32)
    for c in range(nqc):
        lo = slice(c * LANES, (c + 1) * LANES)
        s_ref[:, lo] = s_ref[:, lo] + _causal_add(hpg, tq, c, True)
        hi = slice(NSA_WINDOW + c * LANES, NSA_WINDOW + (c + 1) * LANES)
        s_ref[:, hi] = s_ref[:, hi] + _causal_add(hpg, tq, c, False)
    _online_tile(s_ref, p_ref, m_ref, l_ref, a_ref, acc_ref, vw_ref[pl.ds(d0, wlen), :],
                 rows=rows, width=wlen, first=True)
    ob_ref[2] = _finish(l_ref, acc_ref)

    kd_aug = jnp.concatenate([ks_ref[pl.ds(d0, tq), :], _positions_only(kx_ref[pl.ds(d0, tq), :])], axis=1)
    s_ref[:, 0:tq] = lax.dot_general(q_aug0, kd_aug, _NT, preferred_element_type=F32)
    for c in range(nqc):
        cs = slice(c * LANES, (c + 1) * LANES)
        s_ref[:, cs] = s_ref[:, cs] + _causal_add(hpg, tq, c, False)
    _online_tile(s_ref, p_ref, m_ref, l_ref, a_ref, acc_ref, vs_ref[pl.ds(d0, tq), :],
                 rows=rows, width=tq, first=True)

    def sel_body(kt, carry):
        k0 = pl.multiple_of(kt * tk, tk)
        k_aug = jnp.concatenate([ks_ref[pl.ds(k0, tk), :], kx_ref[pl.ds(k0, tk), :]], axis=1)
        s_ref[:, 0:tk] = lax.dot_general(q_aug, k_aug, _NT, preferred_element_type=F32)
        _online_tile(s_ref, p_ref, m_ref, l_ref, a_ref, acc_ref, vs_ref[pl.ds(k0, tk), :],
                     rows=rows, width=tk, first=False)
        return carry

    lax.fori_loop(0, (t0 + tk - 1) // tk, sel_body, 0)
    ob_ref[1] = _finish(l_ref, acc_ref)

    gts = gt_ref[...].astype(F32)
    gts = pltpu.roll(gts, (LANES - g * hpg) % LANES, 1)
    gts = _sigmoid(gts)
    for h in range(hpg):
        rs = slice(h * tq, (h + 1) * tq)
        cs = slice(h * HEAD_DIM, (h + 1) * HEAD_DIM)
        mix = jnp.zeros((tq, HEAD_DIM), F32)
        for br, z_ref in enumerate((zc_ref, zs_ref, zw_ref)):
            z = z_ref[:, cs].astype(F32)
            gate = gts[:, br * N_HEADS + h:br * N_HEADS + h + 1]
            mix = mix + gate * ob_ref[br, rs, :] * (z * _sigmoid(z))
        o_ref[:, cs] = mix.astype(o_ref.dtype)


def _nsa_attention(pieces, proj, gate_logits, win, kvc, ovl_t, *, batch, seq):
    m = proj.shape[0]
    tq = min(128, seq)
    tk = min(512, seq)
    nt = seq // tq
    nb = seq // NSA_SEL_BLOCK
    nc = seq // NSA_CMP_STRIDE
    hpg, g_n, dh = NSA_HPG, NSA_KV_HEADS, HEAD_DIM
    gw = hpg * dh
    q_cols = N_HEADS * dh
    kv_col0 = q_cols // dh
    z_col0 = (q_cols + 3 * 2 * g_n * dh) // gw
    rows = hpg * tq
    wlen = NSA_WINDOW + tq
    sq = pl.Squeezed()
    assert nb <= POS_COL
    kx = _key_extras(jnp.arange(seq), NSA_SEL_BLOCK)
    wx = _key_extras(jnp.arange(seq + NSA_WINDOW))
    cx = _key_extras(jnp.arange(nc) * NSA_CMP_STRIDE + (NSA_CMP_LEN - 1))

    def kv_spec(branch, kv):
        c = kv_col0 + branch * 2 * g_n + kv * g_n
        return pl.BlockSpec((seq, dh), lambda b, g, i: (b, c + g))

    def z_spec(branch):
        c = z_col0 + branch * g_n
        return pl.BlockSpec((tq, gw), lambda b, g, i: (b * nt + i, c + g))

    win_spec = lambda kv: pl.BlockSpec((sq, seq + NSA_WINDOW, dh), lambda b, g, i: (b, 0, kv * g_n + g))
    cmp_spec = lambda kv: pl.BlockSpec((sq, sq, nc, dh), lambda b, g, i: (kv, b * g_n + g, 0, 0))
    full = lambda arr: pl.BlockSpec(arr.shape, lambda b, g, i: (0,) * arr.ndim)
    in_specs = [
        pl.BlockSpec(memory_space=pltpu.SMEM),
        pl.BlockSpec((tq, gw), lambda b, g, i: (b * nt + i, g)),
        z_spec(0), z_spec(1), z_spec(2),
        pl.BlockSpec((tq, LANES), lambda b, g, i: (b * nt + i, 0)),
        kv_spec(1, 0), kv_spec(1, 1),
        win_spec(0), win_spec(1),
        cmp_spec(0), cmp_spec(1),
        full(kx), full(wx), full(cx), full(ovl_t),
    ]
    kern = functools.partial(_nsa_kernel, tq=tq, tk=tk, n_top=min(NSA_SEL_TOPN, nb))
    return pl.pallas_call(
        kern,
        grid=(batch, g_n, nt),
        in_specs=in_specs,
        out_specs=pl.BlockSpec((tq, gw), lambda b, g, i: (b * nt + i, g)),
        out_shape=jax.ShapeDtypeStruct((m, q_cols), BF16),
        scratch_shapes=[pltpu.VMEM((rows, wlen), F32),
                        pltpu.VMEM((rows, wlen), BF16),
                        pltpu.VMEM((rows, LANES), F32),
                        pltpu.VMEM((rows, LANES), F32),
                        pltpu.VMEM((rows, LANES), F32),
                        pltpu.VMEM((rows, dh), F32),
                        pltpu.VMEM((3, rows, dh), F32)],
        compiler_params=pltpu.CompilerParams(
            dimension_semantics=("arbitrary", "arbitrary", "arbitrary"), vmem_limit_bytes=VMEM_LIMIT),
        name="nsa_attention",
    )(pieces, proj, proj, proj, proj, gate_logits, proj, proj, win, win, kvc, kvc, kx, wx, cx, ovl_t)


def _moba_kernel(sl_ref, q_ref, z_ref, k_ref, v_ref, mx_ref, o_ref,
                 km_ref, s_ref, p_ref, m_ref, l_ref, a_ref, acc_ref, *, tq, tk, n_top):
    g = pl.program_id(1)
    i = pl.program_id(2)
    hpg = MOBA_HPG
    rows = hpg * tq
    seq = k_ref.shape[0]
    nbm = seq // MOBA_BLOCK
    bpt = tk // MOBA_BLOCK

    @pl.when(i == 0)
    def _():
        blk = lax.broadcasted_iota(jnp.int32, (nbm, seq), 1) >> (MOBA_BLOCK.bit_length() - 1)
        avg = jnp.where(blk == lax.broadcasted_iota(jnp.int32, (nbm, seq), 0), 1.0 / MOBA_BLOCK, 0.0)
        km_ref[...] = jnp.dot(avg.astype(BF16), k_ref[...], preferred_element_type=F32)

    qa = _stack_heads(q_ref, hpg)
    qx0 = _slope_extras(sl_ref, g * hpg, hpg, tq).astype(BF16)

    km = km_ref[...]
    km_hi = km.astype(BF16)
    km_lo = (km - km_hi.astype(F32)).astype(BF16)
    sb = (lax.dot_general(km_hi, qa, _NT, preferred_element_type=F32)
          + lax.dot_general(km_lo, qa, _NT, preferred_element_type=F32))
    n_io = lax.broadcasted_iota(jnp.int32, (nbm, rows), 0)
    past = n_io < i
    sb = jnp.where(past, sb, NEG_INF)
    rank = jnp.zeros((nbm, rows), jnp.int32)
    for mm in range(nbm):
        row = sb[mm:mm + 1, :]
        beats = (row > sb) | ((row == sb) & (n_io > mm))
        rank = rank + beats.astype(jnp.int32)
    selneg = jnp.where(past & (rank < n_top), 0.0, NEG_INF).astype(F32)
    qx = qx0 + _pad_rows_to_lanes(selneg).T.astype(BF16)
    q_aug0 = jnp.concatenate([qa, qx0], axis=1)
    q_aug = jnp.concatenate([qa, qx], axis=1)

    d0 = pl.multiple_of(i * tq, tq)
    kd_aug = jnp.concatenate([k_ref[pl.ds(d0, tq), :], _positions_only(mx_ref[pl.ds(d0, tq), :])], axis=1)
    s_ref[:, 0:tq] = lax.dot_general(q_aug0, kd_aug, _NT, preferred_element_type=F32)
    for c in range(tq // LANES):
        cs = slice(c * LANES, (c + 1) * LANES)
        s_ref[:, cs] = s_ref[:, cs] + _causal_add(hpg, tq, c, False)
    _online_tile(s_ref, p_ref, m_ref, l_ref, a_ref, acc_ref, v_ref[pl.ds(d0, tq), :],
                 rows=rows, width=tq, first=True)

    def body(t, carry):
        k0 = pl.multiple_of(t * tk, tk)
        k_aug = jnp.concatenate([k_ref[pl.ds(k0, tk), :], mx_ref[pl.ds(k0, tk), :]], axis=1)
        s_ref[...] = lax.dot_general(q_aug, k_aug, _NT, preferred_element_type=F32)
        _online_tile(s_ref, p_ref, m_ref, l_ref, a_ref, acc_ref, v_ref[pl.ds(k0, tk), :],
                     rows=rows, width=tk, first=False)
        return carry

    lax.fori_loop(0, (i + bpt - 1) // bpt, body, 0)
    o = _finish(l_ref, acc_ref)
    for h in range(hpg):
        cs = slice(h * HEAD_DIM, (h + 1) * HEAD_DIM)
        z = z_ref[:, cs].astype(F32)
        o_ref[:, cs] = (o[h * tq:(h + 1) * tq] * (z * _sigmoid(z))).astype(o_ref.dtype)


def _moba_attention(pieces, qz, kv, *, batch, seq):
    m = qz.shape[0]
    tq = MOBA_BLOCK
    tk = min(2 * MOBA_BLOCK, seq)
    nt = seq // tq
    hpg, g_n, dh = MOBA_HPG, MOBA_KV_HEADS, HEAD_DIM
    gw = hpg * dh
    rows = hpg * tq
    nbm = seq // MOBA_BLOCK
    assert nbm <= POS_COL
    mx = _key_extras(jnp.arange(seq), MOBA_BLOCK)
    in_specs = [
        pl.BlockSpec(memory_space=pltpu.SMEM),
        pl.BlockSpec((tq, gw), lambda b, g, i: (b * nt + i, g)),
        pl.BlockSpec((tq, gw), lambda b, g, i: (b * nt + i, g_n + g)),
        pl.BlockSpec((seq, dh), lambda b, g, i: (b, g)),
        pl.BlockSpec((seq, dh), lambda b, g, i: (b, g_n + g)),
        pl.BlockSpec(mx.shape, lambda b, g, i: (0, 0)),
    ]
    kern = functools.partial(_moba_kernel, tq=tq, tk=tk, n_top=min(MOBA_TOPK, nbm))
    return pl.pallas_call(
        kern,
        grid=(batch, g_n, nt),
        in_specs=in_specs,
        out_specs=pl.BlockSpec((tq, gw), lambda b, g, i: (b * nt + i, g)),
        out_shape=jax.ShapeDtypeStruct((m, N_HEADS * dh), BF16),
        scratch_shapes=[pltpu.VMEM((nbm, dh), F32),
                        pltpu.VMEM((rows, tk), F32),
                        pltpu.VMEM((rows, tk), BF16),
                        pltpu.VMEM((rows, LANES), F32),
                        pltpu.VMEM((rows, LANES), F32),
                        pltpu.VMEM((rows, LANES), F32),
                        pltpu.VMEM((rows, dh), F32)],
        compiler_params=pltpu.CompilerParams(
            dimension_semantics=("arbitrary", "arbitrary", "arbitrary"), vmem_limit_bytes=VMEM_LIMIT),
        name="moba_attention",
    )(pieces, qz, qz, kv, kv, mx)


def _nsa_layer(h, norm_g, w_in, pos_k, pos_v, w1_k, w2_k, w1_v, w2_v, w_out, *, batch, seq):
    dh, g_n = HEAD_DIM, NSA_KV_HEADS
    q_cols = N_HEADS * dh
    kv_cols = 3 * 2 * g_n * dh
    n_main = q_cols + kv_cols + 3 * q_cols
    n_gate = w_in.shape[1] - n_main
    assert n_gate == 3 * N_HEADS <= LANES and n_main % LANES == 0
    (xn,) = _rmsnorm(h, norm_g[None, :], BF16)
    scale = jnp.where(jnp.arange(n_main) < q_cols, dh ** -0.5 * LOG2E, 1.0).astype(F32)[None, :]
    w_t = w_in.T
    proj = _matmul(xn, w_t, out_dtype=BF16, scale=scale, n_cols=n_main, w_is_nk=True,
                   name="nsa_in_proj")
    w_gate_t = jnp.pad(w_t[n_main:], ((0, LANES - n_gate), (0, 0)))
    gate_logits = _matmul(xn, w_gate_t, out_dtype=BF16, scale=jnp.ones((1, LANES), F32), w_is_nk=True,
                          name="nsa_gate_proj")

    half = NSA_CMP_STRIDE
    nr = seq // half
    raw = proj[:, q_cols:q_cols + 2 * g_n * dh].reshape(batch, nr, half, 2, g_n, dh)
    r = raw.transpose(3, 0, 4, 1, 2, 5).reshape(2, batch * g_n, nr, half * dh)
    pos = jnp.stack([pos_k, pos_v])
    pa = pos[:, :half].reshape(2, 1, half * dh)
    pb = pos[:, half:].reshape(2, 1, half * dh)
    w1 = jnp.stack([w1_k, w1_v])
    w1a = w1[:, :half].reshape(2, half * dh, dh).astype(BF16)
    w1b = w1[:, half:].reshape(2, half * dh, dh).astype(BF16)
    w2 = jnp.stack([w2_k, w2_v]).astype(BF16)
    kvc = _compress(r, pa, pb, w1a, w1b, w2)

    wcol = q_cols + 2 * 2 * g_n * dh
    win = proj[:, wcol:wcol + 2 * g_n * dh].reshape(batch, seq, 2 * g_n * dh)
    win = jnp.pad(win, ((0, 0), (NSA_WINDOW, 0), (0, 0)))

    nb = seq // NSA_SEL_BLOCK
    cstart = jnp.arange(nr)[None, :] * NSA_CMP_STRIDE
    sstart = jnp.arange(nb)[:, None] * NSA_SEL_BLOCK
    ovl_t = ((cstart <= sstart + NSA_SEL_BLOCK - 1)
             & (cstart + NSA_CMP_LEN - 1 >= sstart)
             & (jnp.arange(nr)[None, :] < (seq - NSA_CMP_LEN) // NSA_CMP_STRIDE + 1)).astype(BF16)
    mix = _nsa_attention(_slope_pieces(), proj, gate_logits, win, kvc, ovl_t, batch=batch, seq=seq)
    return _matmul(mix, w_out, out_dtype=F32, residual=h, name="nsa_out_proj")


def _moba_shared_kv(xn_kv, kv_w):
    one = jnp.ones((1, kv_w.shape[1]), F32)
    return _matmul(xn_kv, kv_w, out_dtype=BF16, scale=one, name="moba_kv_proj")


def _moba_layer(h, xn_q, kv, w_in, w_out, *, batch, seq):
    dh = HEAD_DIM
    q_cols = N_HEADS * dh
    scale = jnp.where(jnp.arange(w_in.shape[1]) < q_cols, dh ** -0.5 * LOG2E, 1.0).astype(F32)[None, :]
    qz = _matmul(xn_q, w_in, out_dtype=BF16, scale=scale, name="moba_in_proj")
    o = _moba_attention(_slope_pieces(), qz, kv, batch=batch, seq=seq)
    return _matmul(o, w_out, out_dtype=F32, residual=h, name="moba_out_proj")


def kernel(x, a_norm_g, a_w_in, a_cmp_pos_k, a_cmp_pos_v, a_cmp_w1_k, a_cmp_w2_k, a_cmp_w1_v, a_cmp_w2_v,
           a_w_out, kv_norm_g, kv_w, b_norm_g, b_w_in, b_w_out, final_norm_g):
    batch, seq, d = x.shape
    h = x.reshape(batch * seq, d)
    for layer in range(a_norm_g.shape[0]):
        h = _nsa_layer(h, a_norm_g[layer], a_w_in[layer], a_cmp_pos_k[layer], a_cmp_pos_v[layer],
                       a_cmp_w1_k[layer], a_cmp_w2_k[layer], a_cmp_w1_v[layer], a_cmp_w2_v[layer],
                       a_w_out[layer], batch=batch, seq=seq)
    kv = None
    for layer in range(b_norm_g.shape[0]):
        if layer == 0:
            xn_kv, xn_q = _rmsnorm(h, jnp.stack([kv_norm_g, b_norm_g[layer]]), BF16)
            kv = _moba_shared_kv(xn_kv, kv_w)
        else:
            (xn_q,) = _rmsnorm(h, b_norm_g[layer][None, :], BF16)
        h = _moba_layer(h, xn_q, kv, b_w_in[layer], b_w_out[layer], batch=batch, seq=seq)
    (out,) = _rmsnorm(h, final_norm_g[None, :], F32)
    return out.reshape(batch, seq, d)
```

```python
import functools

import jax
import jax.numpy as jnp
from jax import lax
from jax.experimental import pallas as pl
from jax.experimental.pallas import tpu as pltpu

F32 = jnp.float32
BF16 = jnp.bfloat16

N_HEADS = 32
HEAD_DIM = 128
NSA_KV_HEADS = 4
NSA_HPG = N_HEADS // NSA_KV_HEADS
NSA_CMP_LEN = 32
NSA_CMP_STRIDE = 16
NSA_SEL_BLOCK = 64
NSA_SEL_TOPN = 16
NSA_WINDOW = 512
MOBA_KV_HEADS = 8
MOBA_HPG = N_HEADS // MOBA_KV_HEADS
MOBA_BLOCK = 256
MOBA_TOPK = 3
RMS_EPS = 1e-6
NEG_INF = -1e30
FORCE_SCORE = 1e9
LOG2E = 1.4426950408889634

LANES = 128
VMEM_LIMIT = 56 * 1024 * 1024
NSA_GROUPS_PER_STEP = 2
MOBA_GROUPS_PER_STEP = 4
POS_COL = 120
POS_SPLIT = 64

_NT = (((1,), (1,)), ((), ()))


def _sigmoid(x):
    return 1.0 / (1.0 + jnp.exp(-x))


def _rmsnorm_kernel(x_ref, g_ref, *o_refs):
    x = x_ref[...].astype(F32)
    y = x * lax.rsqrt(jnp.mean(x * x, axis=-1, keepdims=True) + RMS_EPS)
    for i, o_ref in enumerate(o_refs):
        o_ref[...] = (y * g_ref[i:i + 1, :]).astype(o_ref.dtype)


def _rmsnorm(x, gains, out_dtype):
    m, d = x.shape
    n = gains.shape[0]
    tm = min(256, m)
    outs = pl.pallas_call(
        _rmsnorm_kernel,
        grid=(m // tm,),
        in_specs=[pl.BlockSpec((tm, d), lambda i: (i, 0)),
                  pl.BlockSpec((n, d), lambda i: (0, 0))],
        out_specs=[pl.BlockSpec((tm, d), lambda i: (i, 0)) for _ in range(n)],
        out_shape=[jax.ShapeDtypeStruct((m, d), out_dtype) for _ in range(n)],
        compiler_params=pltpu.CompilerParams(
            dimension_semantics=("arbitrary",), vmem_limit_bytes=VMEM_LIMIT),
        name="rmsnorm",
    )(x, gains)
    return outs


def _mm_kernel(a_ref, w_ref, x_ref, o_ref, *, w_is_nk, residual):
    w = w_ref[...].astype(BF16)
    if w_is_nk:
        acc = lax.dot_general(a_ref[...], w, _NT, preferred_element_type=F32)
    else:
        acc = jnp.dot(a_ref[...], w, preferred_element_type=F32)
    o_ref[...] = ((x_ref[...] + acc) if residual else (acc * x_ref[...])).astype(o_ref.dtype)


def _matmul(a, w, *, out_dtype, scale=None, residual=None, n_cols=None, w_is_nk=False, name):
    m, k = a.shape
    n = (w.shape[0] if w_is_nk else w.shape[1]) if n_cols is None else n_cols
    tn = min(512, n)
    assert n % tn == 0
    tm = min(1024, m)
    a_spec = pl.BlockSpec((tm, k), lambda i, j: (i, 0))
    w_spec = pl.BlockSpec((tn, k), lambda i, j: (j, 0)) if w_is_nk else pl.BlockSpec((k, tn), lambda i, j: (0, j))
    o_spec = pl.BlockSpec((tm, tn), lambda i, j: (i, j))
    kern = functools.partial(_mm_kernel, w_is_nk=w_is_nk, residual=residual is not None)
    if residual is None:
        extra, extra_spec = scale, pl.BlockSpec((1, tn), lambda i, j: (0, j))
    else:
        extra, extra_spec = residual, o_spec
    return pl.pallas_call(
        kern,
        grid=(m // tm, n // tn),
        in_specs=[a_spec, w_spec, extra_spec],
        out_specs=o_spec,
        out_shape=jax.ShapeDtypeStruct((m, n), out_dtype),
        compiler_params=pltpu.CompilerParams(
            dimension_semantics=("arbitrary", "arbitrary"), vmem_limit_bytes=VMEM_LIMIT),
        name=name,
    )(a, w, extra)


def _cmp_kernel(r_ref, pa_ref, pb_ref, w1a_ref, w1b_ref, w2_ref, o_ref):
    r = r_ref[...].astype(F32)
    nr = r.shape[0]
    xa = (r + pa_ref[...]).astype(BF16)
    xb = (r + pb_ref[...]).astype(BF16)
    ya = jnp.dot(xa, w1a_ref[...], preferred_element_type=F32)
    yb = jnp.dot(xb, w1b_ref[...], preferred_element_type=F32)
    hid = ya + pltpu.roll(yb, nr - 1, 0)
    hid = hid * _sigmoid(hid)
    o_ref[...] = jnp.dot(hid.astype(BF16), w2_ref[...], preferred_element_type=F32).astype(o_ref.dtype)


def _compress(r, pa, pb, w1a, w1b, w2):
    two, bg, nr, kk = r.shape
    dh = w2.shape[-1]
    sq = pl.Squeezed()
    wspec = lambda shape: pl.BlockSpec((sq,) + shape, lambda t, i: (t, 0, 0))
    return pl.pallas_call(
        _cmp_kernel,
        grid=(two, bg),
        in_specs=[pl.BlockSpec((sq, sq, nr, kk), lambda t, i: (t, i, 0, 0)),
                  wspec((1, kk)), wspec((1, kk)), wspec((kk, dh)), wspec((kk, dh)), wspec((dh, dh))],
        out_specs=pl.BlockSpec((sq, sq, nr, dh), lambda t, i: (t, i, 0, 0)),
        out_shape=jax.ShapeDtypeStruct((two, bg, nr, dh), BF16),
        compiler_params=pltpu.CompilerParams(
            dimension_semantics=("arbitrary", "arbitrary"), vmem_limit_bytes=VMEM_LIMIT),
        name="nsa_compress",
    )(r, pa, pb, w1a, w1b, w2)


def _stack_heads(ref, n_heads, col0=0):
    return jnp.concatenate(
        [ref[:, col0 + h * HEAD_DIM:col0 + (h + 1) * HEAD_DIM] for h in range(n_heads)], axis=0)


def _query_in_tile(n_heads, tq, width):
    assert tq & (tq - 1) == 0, "query tile must be a power of two"
    return lax.broadcasted_iota(jnp.int32, (n_heads * tq, width), 0) & (tq - 1)


def _slope_extras(sl_ref, head0, n_heads, tq):
    lane = lax.broadcasted_iota(jnp.int32, (tq, LANES), 1)
    blocks = []
    for h in range(n_heads):
        x = jnp.zeros((tq, LANES), F32)
        for piece in range(3):
            hit = (lane == POS_COL + piece) | (lane == POS_COL + 3 + piece)
            x = jnp.where(hit, sl_ref[piece, head0 + h], x)
        blocks.append(x)
    return jnp.concatenate(blocks, axis=0)


def _causal_add(n_heads, tq, chunk, strict_future):
    qq = _query_in_tile(n_heads, tq, LANES)
    kk = lax.broadcasted_iota(jnp.int32, (n_heads * tq, LANES), 1) + chunk * LANES
    keep = (kk > qq) if strict_future else (kk <= qq)
    return jnp.where(keep, 0.0, NEG_INF)


def _positions_only(kx_tile):
    lane = lax.broadcasted_iota(jnp.int32, kx_tile.shape, 1)
    return jnp.where(lane >= POS_COL, kx_tile, jnp.zeros_like(kx_tile))


def _pad_rows_to_lanes(x):
    n = x.shape[0]
    assert n <= POS_COL, "block one-hot columns must stay clear of the slope/position columns"
    return jnp.concatenate([x, jnp.zeros((LANES - n, x.shape[1]), x.dtype)], axis=0)


def _online_tile(s_ref, p_ref, m_ref, l_ref, a_ref, acc_ref, v, *, rows, width, first):
    nch = width // LANES
    mt = s_ref[:, 0:LANES]
    for c in range(1, nch):
        mt = jnp.maximum(mt, s_ref[:, c * LANES:(c + 1) * LANES])
    mrow = jnp.max(mt, axis=-1, keepdims=True)
    if first:
        m_new = jnp.broadcast_to(mrow, (rows, LANES))
    else:
        m_old = m_ref[...]
        m_new = jnp.maximum(m_old, mrow)
        a_ref[...] = jnp.exp2(m_old - m_new)
    m_ref[...] = m_new
    ps = None
    for c in range(nch):
        cs = slice(c * LANES, (c + 1) * LANES)
        p = jnp.exp2(s_ref[:, cs] - m_new)
        p_ref[:, cs] = p.astype(BF16)
        ps = p if ps is None else ps + p
    if first:
        l_ref[...] = ps
    else:
        l_ref[...] = a_ref[...] * l_ref[...] + ps
    pv = jnp.dot(p_ref[:, 0:width], v, preferred_element_type=F32)
    if first:
        acc_ref[...] = pv
    else:
        acc_ref[...] = a_ref[...] * acc_ref[...] + pv


def _finish(l_ref, acc_ref):
    return acc_ref[...] * (1.0 / jnp.sum(l_ref[...], axis=-1, keepdims=True))


def _key_extras(pos, block=None):
    pos = pos[:, None]
    col = jnp.arange(LANES)[None, :]
    x = jnp.zeros((pos.shape[0], LANES), F32)
    if block is not None:
        x = jnp.where(col == pos // block, 1.0, x)
    x = jnp.where((col >= POS_COL) & (col < POS_COL + 3), (pos // POS_SPLIT) * POS_SPLIT, x)
    x = jnp.where((col >= POS_COL + 3) & (col < POS_COL + 6), pos % POS_SPLIT, x)
    return x.astype(BF16)


def _slope_pieces():
    s = 2.0 ** (-8.0 * jnp.arange(1, N_HEADS + 1, dtype=F32) / N_HEADS) * LOG2E
    a = s.astype(BF16).astype(F32)
    b = (s - a).astype(BF16).astype(F32)
    c = (s - a - b).astype(BF16).astype(F32)
    return jnp.stack([a, b, c])


def _nsa_kernel(sl_ref, q_ref, *refs, tq, tk, n_top, gps):
    z_refs = [refs[br * gps:(br + 1) * gps] for br in range(3)]
    (gt_ref, ks_ref, vs_ref, kw_ref, vw_ref, kc_ref, vc_ref, kx_ref, wx_ref, cx_ref, ovl_ref, o_ref,
     s_ref, p_ref, m_ref, l_ref, a_ref, acc_ref, ob_ref) = refs[3 * gps:]
    gp = pl.program_id(1)
    i = pl.program_id(2)
    t0 = i * tq
    hpg = NSA_HPG
    gw = hpg * HEAD_DIM
    rows = hpg * tq
    grp = range(gps)
    kcol = lambda u: slice(u * HEAD_DIM, (u + 1) * HEAD_DIM)
    stats = lambda u: (s_ref.at[u], p_ref.at[u], m_ref.at[u], l_ref.at[u], a_ref.at[u], acc_ref.at[u])
    nqc = tq // LANES
    d0 = pl.multiple_of(t0, tq)
    qa = [_stack_heads(q_ref, hpg, u * gw) for u in grp]
    qx0 = [_slope_extras(sl_ref, (gp * gps + u) * hpg, hpg, tq).astype(BF16) for u in grp]
    q_aug0 = [jnp.concatenate([qa[u], qx0[u]], axis=1) for u in grp]

    nc = kc_ref.shape[1]
    cx = cx_ref[...]
    cend = lax.broadcasted_iota(jnp.int32, (1, nc), 1) * NSA_CMP_STRIDE + (NSA_CMP_LEN - 1)
    tqv = t0 + (lax.broadcasted_iota(jnp.int32, (rows, 1), 0) & (tq - 1))
    mask = cend <= tqv
    ovl = ovl_ref[...]
    nb = ovl.shape[0]
    j = lax.broadcasted_iota(jnp.int32, (nb, tq), 0)
    blkq = (t0 + lax.broadcasted_iota(jnp.int32, (nb, tq), 1)) >> (NSA_SEL_BLOCK.bit_length() - 1)
    forced = (j == 0) | (j == blkq) | (j == blkq - 1)
    blk0 = t0 >> (NSA_SEL_BLOCK.bit_length() - 1)
    q_aug = []
    for u in grp:
        kc_aug = jnp.concatenate([kc_ref[u], cx], axis=1)
        s = lax.dot_general(q_aug0[u], kc_aug, _NT, preferred_element_type=F32)
        s = jnp.where(mask, s, NEG_INF)
        mx = jnp.max(s, axis=-1, keepdims=True)
        e = jnp.where(mask, jnp.exp2(s - mx), 0.0)
        den = jnp.sum(e, axis=-1, keepdims=True)
        p = e * (1.0 / jnp.where(den > 0, den, 1.0))
        ob_ref[u, 0] = jnp.dot(p.astype(BF16), vc_ref[u], preferred_element_type=F32)
        psum = p[0:tq]
        for h in range(1, hpg):
            psum = psum + p[h * tq:(h + 1) * tq]
        p_hi = psum.astype(BF16)
        p_lo = (psum - p_hi.astype(F32)).astype(BF16)
        imp = (lax.dot_general(ovl, p_hi, _NT, preferred_element_type=F32)
               + lax.dot_general(ovl, p_lo, _NT, preferred_element_type=F32))
        imp = jnp.where(forced, FORCE_SCORE, jnp.where(j > blkq, NEG_INF, imp))
        rank = jnp.zeros((nb, tq), jnp.int32)
        for ii in range(nb):
            row = imp[ii:ii + 1, :]
            beats = (row > imp) | ((row == imp) & (j > ii))
            rank = rank + beats.astype(jnp.int32)
        selneg = jnp.where((rank < n_top) & (j < blk0), 0.0, NEG_INF).astype(F32)
        selneg_q = _pad_rows_to_lanes(selneg).T.astype(BF16)
        qx = qx0[u] + jnp.concatenate([selneg_q] * hpg, axis=0)
        q_aug.append(jnp.concatenate([qa[u], qx], axis=1))

    wlen = NSA_WINDOW + tq
    wxt = wx_ref[pl.ds(d0, wlen), :]
    for u in grp:
        su = s_ref.at[u]
        kw_aug = jnp.concatenate([kw_ref[pl.ds(d0, wlen), kcol(u)], wxt], axis=1)
        su[:, 0:wlen] = lax.dot_general(q_aug0[u], kw_aug, _NT, preferred_element_type=F32)
        for c in range(nqc):
            lo = slice(c * LANES, (c + 1) * LANES)
            su[:, lo] = su[:, lo] + _causal_add(hpg, tq, c, True)
            hi = slice(NSA_WINDOW + c * LANES, NSA_WINDOW + (c + 1) * LANES)
            su[:, hi] = su[:, hi] + _causal_add(hpg, tq, c, False)
    for u in grp:
        _online_tile(*stats(u), vw_ref[pl.ds(d0, wlen), kcol(u)], rows=rows, width=wlen, first=True)
        ob_ref[u, 2] = _finish(l_ref.at[u], acc_ref.at[u])

    kxd = _positions_only(kx_ref[pl.ds(d0, tq), :])
    for u in grp:
        su = s_ref.at[u]
        kd_aug = jnp.concatenate([ks_ref[pl.ds(d0, tq), kcol(u)], kxd], axis=1)
        su[:, 0:tq] = lax.dot_general(q_aug0[u], kd_aug, _NT, preferred_element_type=F32)
        for c in range(nqc):
            cs = slice(c * LANES, (c + 1) * LANES)
            su[:, cs] = su[:, cs] + _causal_add(hpg, tq, c, False)
    for u in grp:
        _online_tile(*stats(u), vs_ref[pl.ds(d0, tq), kcol(u)], rows=rows, width=tq, first=True)

    def sel_body(kt, carry):
        k0 = pl.multiple_of(kt * tk, tk)
        kxt = kx_ref[pl.ds(k0, tk), :]
        for u in grp:
            k_aug = jnp.concatenate([ks_ref[pl.ds(k0, tk), kcol(u)], kxt], axis=1)
            s_ref[u, :, 0:tk] = lax.dot_general(q_aug[u], k_aug, _NT, preferred_element_type=F32)
        for u in grp:
            _online_tile(*stats(u), vs_ref[pl.ds(k0, tk), kcol(u)], rows=rows, width=tk, first=False)
        return carry

    lax.fori_loop(0, (t0 + tk - 1) // tk, sel_body, 0)
    for u in grp:
        ob_ref[u, 1] = _finish(l_ref.at[u], acc_ref.at[u])

    gates = gt_ref[...].astype(F32)
    for u in grp:
        gts = pltpu.roll(gates, (LANES - (gp * gps + u) * hpg) % LANES, 1)
        gts = _sigmoid(gts)
        for h in range(hpg):
            rs = slice(h * tq, (h + 1) * tq)
            cs = slice(h * HEAD_DIM, (h + 1) * HEAD_DIM)
            mix = jnp.zeros((tq, HEAD_DIM), F32)
            for br in range(3):
                z = z_refs[br][u][:, cs].astype(F32)
                gate = gts[:, br * N_HEADS + h:br * N_HEADS + h + 1]
                mix = mix + gate * ob_ref[u, br, rs, :] * (z * _sigmoid(z))
            o_ref[:, u * gw + h * HEAD_DIM:u * gw + (h + 1) * HEAD_DIM] = mix.astype(o_ref.dtype)


def _nsa_attention(pieces, proj, gate_logits, win, kvc, ovl_t, *, batch, seq):
    m = proj.shape[0]
    tq = min(128, seq)
    tk = min(512, seq)
    nt = seq // tq
    nb = seq // NSA_SEL_BLOCK
    nc = seq // NSA_CMP_STRIDE
    hpg, g_n, dh = NSA_HPG, NSA_KV_HEADS, HEAD_DIM
    gps = NSA_GROUPS_PER_STEP
    ngp = g_n // gps
    gw = hpg * dh
    q_cols = N_HEADS * dh
    kv_col0 = q_cols // (gps * dh)
    z_col0 = (q_cols + 3 * 2 * g_n * dh) // gw
    rows = hpg * tq
    wlen = NSA_WINDOW + tq
    sq = pl.Squeezed()
    assert nb <= POS_COL and g_n % gps == 0
    kx = _key_extras(jnp.arange(seq), NSA_SEL_BLOCK)
    wx = _key_extras(jnp.arange(seq + NSA_WINDOW))
    cx = _key_extras(jnp.arange(nc) * NSA_CMP_STRIDE + (NSA_CMP_LEN - 1))

    def kv_spec(branch, kv):
        c = kv_col0 + (branch * 2 + kv) * ngp
        return pl.BlockSpec((seq, gps * dh), lambda b, g, i: (b, c + g))

    def z_spec(branch, u):
        c = z_col0 + branch * g_n + u
        return pl.BlockSpec((tq, gw), lambda b, g, i: (b * nt + i, c + g * gps))

    win_spec = lambda kv: pl.BlockSpec((sq, seq + NSA_WINDOW, gps * dh), lambda b, g, i: (b, 0, kv * ngp + g))
    cmp_spec = lambda kv: pl.BlockSpec((sq, gps, nc, dh), lambda b, g, i: (kv, b * ngp + g, 0, 0))
    full = lambda arr: pl.BlockSpec(arr.shape, lambda b, g, i: (0,) * arr.ndim)
    in_specs = (
        [pl.BlockSpec(memory_space=pltpu.SMEM),
         pl.BlockSpec((tq, gps * gw), lambda b, g, i: (b * nt + i, g))]
        + [z_spec(br, u) for br in range(3) for u in range(gps)]
        + [pl.BlockSpec((tq, LANES), lambda b, g, i: (b * nt + i, 0)),
           kv_spec(1, 0), kv_spec(1, 1),
           win_spec(0), win_spec(1),
           cmp_spec(0), cmp_spec(1),
           full(kx), full(wx), full(cx), full(ovl_t)])
    kern = functools.partial(_nsa_kernel, tq=tq, tk=tk, n_top=min(NSA_SEL_TOPN, nb), gps=gps)
    return pl.pallas_call(
        kern,
        grid=(batch, ngp, nt),
        in_specs=in_specs,
        out_specs=pl.BlockSpec((tq, gps * gw), lambda b, g, i: (b * nt + i, g)),
        out_shape=jax.ShapeDtypeStruct((m, q_cols), BF16),
        scratch_shapes=[pltpu.VMEM((gps, rows, wlen), F32),
                        pltpu.VMEM((gps, rows, wlen), BF16),
                        pltpu.VMEM((gps, rows, LANES), F32),
                        pltpu.VMEM((gps, rows, LANES), F32),
                        pltpu.VMEM((gps, rows, LANES), F32),
                        pltpu.VMEM((gps, rows, dh), F32),
                        pltpu.VMEM((gps, 3, rows, dh), F32)],
        compiler_params=pltpu.CompilerParams(
            dimension_semantics=("arbitrary", "arbitrary", "arbitrary"), vmem_limit_bytes=VMEM_LIMIT),
        name="nsa_attention",
    )(pieces, proj, *([proj] * (3 * gps)), gate_logits, proj, proj, win, win, kvc, kvc, kx, wx, cx, ovl_t)


def _moba_kernel(sl_ref, q_ref, z_ref, k_ref, v_ref, mx_ref, o_ref,
                 km_ref, s_ref, p_ref, m_ref, l_ref, a_ref, acc_ref, *, tq, tk, n_top, gps):
    gp = pl.program_id(1)
    i = pl.program_id(2)
    hpg = MOBA_HPG
    gw = hpg * HEAD_DIM
    rows = hpg * tq
    seq = k_ref.shape[0]
    nbm = seq // MOBA_BLOCK
    bpt = tk // MOBA_BLOCK
    grp = range(gps)
    kcol = lambda u: slice(u * HEAD_DIM, (u + 1) * HEAD_DIM)
    stats = lambda u: (s_ref.at[u], p_ref.at[u], m_ref.at[u], l_ref.at[u], a_ref.at[u], acc_ref.at[u])

    @pl.when(i == 0)
    def _():
        blk = lax.broadcasted_iota(jnp.int32, (nbm, seq), 1) >> (MOBA_BLOCK.bit_length() - 1)
        avg = jnp.where(blk == lax.broadcasted_iota(jnp.int32, (nbm, seq), 0), 1.0 / MOBA_BLOCK, 0.0)
        for u in grp:
            km_ref[u] = jnp.dot(avg.astype(BF16), k_ref[:, kcol(u)], preferred_element_type=F32)

    n_io = lax.broadcasted_iota(jnp.int32, (nbm, rows), 0)
    past = n_io < i
    q_aug0, q_aug = [], []
    for u in grp:
        qa = _stack_heads(q_ref, hpg, u * gw)
        qx0 = _slope_extras(sl_ref, (gp * gps + u) * hpg, hpg, tq).astype(BF16)
        km = km_ref[u]
        km_hi = km.astype(BF16)
        km_lo = (km - km_hi.astype(F32)).astype(BF16)
        sb = (lax.dot_general(km_hi, qa, _NT, preferred_element_type=F32)
              + lax.dot_general(km_lo, qa, _NT, preferred_element_type=F32))
        sb = jnp.where(past, sb, NEG_INF)
        rank = jnp.zeros((nbm, rows), jnp.int32)
        for mm in range(nbm):
            row = sb[mm:mm + 1, :]
            beats = (row > sb) | ((row == sb) & (n_io > mm))
            rank = rank + beats.astype(jnp.int32)
        selneg = jnp.where(past & (rank < n_top), 0.0, NEG_INF).astype(F32)
        qx = qx0 + _pad_rows_to_lanes(selneg).T.astype(BF16)
        q_aug0.append(jnp.concatenate([qa, qx0], axis=1))
        q_aug.append(jnp.concatenate([qa, qx], axis=1))

    d0 = pl.multiple_of(i * tq, tq)
    mxd = _positions_only(mx_ref[pl.ds(d0, tq), :])
    for u in grp:
        su = s_ref.at[u]
        kd_aug = jnp.concatenate([k_ref[pl.ds(d0, tq), kcol(u)], mxd], axis=1)
        su[:, 0:tq] = lax.dot_general(q_aug0[u], kd_aug, _NT, preferred_element_type=F32)
        for c in range(tq // LANES):
            cs = slice(c * LANES, (c + 1) * LANES)
            su[:, cs] = su[:, cs] + _causal_add(hpg, tq, c, False)
    for u in grp:
        _online_tile(*stats(u), v_ref[pl.ds(d0, tq), kcol(u)], rows=rows, width=tq, first=True)

    def body(t, carry):
        k0 = pl.multiple_of(t * tk, tk)
        mxt = mx_ref[pl.ds(k0, tk), :]
        for u in grp:
            k_aug = jnp.concatenate([k_ref[pl.ds(k0, tk), kcol(u)], mxt], axis=1)
            s_ref[u] = lax.dot_general(q_aug[u], k_aug, _NT, preferred_element_type=F32)
        for u in grp:
            _online_tile(*stats(u), v_ref[pl.ds(k0, tk), kcol(u)], rows=rows, width=tk, first=False)
        return carry

    lax.fori_loop(0, (i + bpt - 1) // bpt, body, 0)
    for u in grp:
        o = _finish(l_ref.at[u], acc_ref.at[u])
        for h in range(hpg):
            cs = slice(u * gw + h * HEAD_DIM, u * gw + (h + 1) * HEAD_DIM)
            z = z_ref[:, cs].astype(F32)
            o_ref[:, cs] = (o[h * tq:(h + 1) * tq] * (z * _sigmoid(z))).astype(o_ref.dtype)


def _moba_attention(pieces, qz, kv, *, batch, seq):
    m = qz.shape[0]
    tq = MOBA_BLOCK
    tk = min(2 * MOBA_BLOCK, seq)
    nt = seq // tq
    hpg, g_n, dh = MOBA_HPG, MOBA_KV_HEADS, HEAD_DIM
    gps = MOBA_GROUPS_PER_STEP
    ngp = g_n // gps
    gw = gps * hpg * dh
    rows = hpg * tq
    nbm = seq // MOBA_BLOCK
    assert nbm <= POS_COL and g_n % gps == 0
    mx = _key_extras(jnp.arange(seq), MOBA_BLOCK)
    in_specs = [
        pl.BlockSpec(memory_space=pltpu.SMEM),
        pl.BlockSpec((tq, gw), lambda b, g, i: (b * nt + i, g)),
        pl.BlockSpec((tq, gw), lambda b, g, i: (b * nt + i, ngp + g)),
        pl.BlockSpec((seq, gps * dh), lambda b, g, i: (b, g)),
        pl.BlockSpec((seq, gps * dh), lambda b, g, i: (b, ngp + g)),
        pl.BlockSpec(mx.shape, lambda b, g, i: (0, 0)),
    ]
    kern = functools.partial(_moba_kernel, tq=tq, tk=tk, n_top=min(MOBA_TOPK, nbm), gps=gps)
    return pl.pallas_call(
        kern,
        grid=(batch, ngp, nt),
        in_specs=in_specs,
        out_specs=pl.BlockSpec((tq, gw), lambda b, g, i: (b * nt + i, g)),
        out_shape=jax.ShapeDtypeStruct((m, N_HEADS * dh), BF16),
        scratch_shapes=[pltpu.VMEM((gps, nbm, dh), F32),
                        pltpu.VMEM((gps, rows, tk), F32),
                        pltpu.VMEM((gps, rows, tk), BF16),
                        pltpu.VMEM((gps, rows, LANES), F32),
                        pltpu.VMEM((gps, rows, LANES), F32),
                        pltpu.VMEM((gps, rows, LANES), F32),
                        pltpu.VMEM((gps, rows, dh), F32)],
        compiler_params=pltpu.CompilerParams(
            dimension_semantics=("arbitrary", "arbitrary", "arbitrary"), vmem_limit_bytes=VMEM_LIMIT),
        name="moba_attention",
    )(pieces, qz, qz, kv, kv, mx)


def _nsa_layer(h, norm_g, w_in, pos_k, pos_v, w1_k, w2_k, w1_v, w2_v, w_out, *, batch, seq):
    dh, g_n = HEAD_DIM, NSA_KV_HEADS
    q_cols = N_HEADS * dh
    kv_cols = 3 * 2 * g_n * dh
    n_main = q_cols + kv_cols + 3 * q_cols
    n_gate = w_in.shape[1] - n_main
    assert n_gate == 3 * N_HEADS <= LANES and n_main % LANES == 0
    (xn,) = _rmsnorm(h, norm_g[None, :], BF16)
    scale = jnp.where(jnp.arange(n_main) < q_cols, dh ** -0.5 * LOG2E, 1.0).astype(F32)[None, :]
    w_t = w_in.T
    proj = _matmul(xn, w_t, out_dtype=BF16, scale=scale, n_cols=n_main, w_is_nk=True,
                   name="nsa_in_proj")
    w_gate_t = jnp.pad(w_t[n_main:], ((0, LANES - n_gate), (0, 0)))
    gate_logits = _matmul(xn, w_gate_t, out_dtype=BF16, scale=jnp.ones((1, LANES), F32), w_is_nk=True,
                          name="nsa_gate_proj")

    half = NSA_CMP_STRIDE
    nr = seq // half
    raw = proj[:, q_cols:q_cols + 2 * g_n * dh].reshape(batch, nr, half, 2, g_n, dh)
    r = raw.transpose(3, 0, 4, 1, 2, 5).reshape(2, batch * g_n, nr, half * dh)
    pos = jnp.stack([pos_k, pos_v])
    pa = pos[:, :half].reshape(2, 1, half * dh)
    pb = pos[:, half:].reshape(2, 1, half * dh)
    w1 = jnp.stack([w1_k, w1_v])
    w1a = w1[:, :half].reshape(2, half * dh, dh).astype(BF16)
    w1b = w1[:, half:].reshape(2, half * dh, dh).astype(BF16)
    w2 = jnp.stack([w2_k, w2_v]).astype(BF16)
    kvc = _compress(r, pa, pb, w1a, w1b, w2)

    wcol = q_cols + 2 * 2 * g_n * dh
    win = proj[:, wcol:wcol + 2 * g_n * dh].reshape(batch, seq, 2 * g_n * dh)
    win = jnp.pad(win, ((0, 0), (NSA_WINDOW, 0), (0, 0)))

    nb = seq // NSA_SEL_BLOCK
    cstart = jnp.arange(nr)[None, :] * NSA_CMP_STRIDE
    sstart = jnp.arange(nb)[:, None] * NSA_SEL_BLOCK
    ovl_t = ((cstart <= sstart + NSA_SEL_BLOCK - 1)
             & (cstart + NSA_CMP_LEN - 1 >= sstart)
             & (jnp.arange(nr)[None, :] < (seq - NSA_CMP_LEN) // NSA_CMP_STRIDE + 1)).astype(BF16)
    mix = _nsa_attention(_slope_pieces(), proj, gate_logits, win, kvc, ovl_t, batch=batch, seq=seq)
    return _matmul(mix, w_out, out_dtype=F32, residual=h, name="nsa_out_proj")


def _moba_shared_kv(xn_kv, kv_w):
    one = jnp.ones((1, kv_w.shape[1]), F32)
    return _matmul(xn_kv, kv_w, out_dtype=BF16, scale=one, name="moba_kv_proj")


def _moba_layer(h, xn_q, kv, w_in, w_out, *, batch, seq):
    dh = HEAD_DIM
    q_cols = N_HEADS * dh
    scale = jnp.where(jnp.arange(w_in.shape[1]) < q_cols, dh ** -0.5 * LOG2E, 1.0).astype(F32)[None, :]
    qz = _matmul(xn_q, w_in, out_dtype=BF16, scale=scale, name="moba_in_proj")
    o = _moba_attention(_slope_pieces(), qz, kv, batch=batch, seq=seq)
    return _matmul(o, w_out, out_dtype=F32, residual=h, name="moba_out_proj")


def kernel(x, a_norm_g, a_w_in, a_cmp_pos_k, a_cmp_pos_v, a_cmp_w1_k, a_cmp_w2_k, a_cmp_w1_v, a_cmp_w2_v,
           a_w_out, kv_norm_g, kv_w, b_norm_g, b_w_in, b_w_out, final_norm_g):
    batch, seq, d = x.shape
    h = x.reshape(batch * seq, d)
    for layer in range(a_norm_g.shape[0]):
        h = _nsa_layer(h, a_norm_g[layer], a_w_in[layer], a_cmp_pos_k[layer], a_cmp_pos_v[layer],
                       a_cmp_w1_k[layer], a_cmp_w2_k[layer], a_cmp_w1_v[layer], a_cmp_w2_v[layer],
                       a_w_out[layer], batch=batch, seq=seq)
    kv = None
    for layer in range(b_norm_g.shape[0]):
        if layer == 0:
            xn_kv, xn_q = _rmsnorm(h, jnp.stack([kv_norm_g, b_norm_g[layer]]), BF16)
            kv = _moba_shared_kv(xn_kv, kv_w)
        else:
            (xn_q,) = _rmsnorm(h, b_norm_g[layer][None, :], BF16)
        h = _moba_layer(h, xn_q, kv, b_w_in[layer], b_w_out[layer], batch=batch, seq=seq)
    (out,) = _rmsnorm(h, final_norm_g[None, :], F32)
    return out.reshape(batch, seq, d)
```

```python
import functools

import jax
import jax.numpy as jnp
from jax import lax
from jax.experimental import pallas as pl
from jax.experimental.pallas import tpu as pltpu

F32 = jnp.float32
BF16 = jnp.bfloat16

N_HEADS = 32
HEAD_DIM = 128
NSA_KV_HEADS = 4
NSA_HPG = N_HEADS // NSA_KV_HEADS
NSA_CMP_LEN = 32
NSA_CMP_STRIDE = 16
NSA_SEL_BLOCK = 64
NSA_SEL_TOPN = 16
NSA_WINDOW = 512
MOBA_KV_HEADS = 8
MOBA_HPG = N_HEADS // MOBA_KV_HEADS
MOBA_BLOCK = 256
MOBA_TOPK = 3
RMS_EPS = 1e-6
NEG_INF = -1e30
FORCE_SCORE = 1e9
LOG2E = 1.4426950408889634

LANES = 128
VMEM_LIMIT = 56 * 1024 * 1024
NSA_GROUPS_PER_STEP = 2
MOBA_GROUPS_PER_STEP = 4
POS_COL = 120
POS_SPLIT = 64

_NT = (((1,), (1,)), ((), ()))


def _sigmoid(x):
    return 1.0 / (1.0 + jnp.exp(-x))


def _rmsnorm_kernel(x_ref, g_ref, *o_refs):
    x = x_ref[...].astype(F32)
    y = x * lax.rsqrt(jnp.mean(x * x, axis=-1, keepdims=True) + RMS_EPS)
    for i, o_ref in enumerate(o_refs):
        o_ref[...] = (y * g_ref[i:i + 1, :]).astype(o_ref.dtype)


def _rmsnorm(x, gains, out_dtype):
    m, d = x.shape
    n = gains.shape[0]
    tm = min(256, m)
    outs = pl.pallas_call(
        _rmsnorm_kernel,
        grid=(m // tm,),
        in_specs=[pl.BlockSpec((tm, d), lambda i: (i, 0)),
                  pl.BlockSpec((n, d), lambda i: (0, 0))],
        out_specs=[pl.BlockSpec((tm, d), lambda i: (i, 0)) for _ in range(n)],
        out_shape=[jax.ShapeDtypeStruct((m, d), out_dtype) for _ in range(n)],
        compiler_params=pltpu.CompilerParams(
            dimension_semantics=("arbitrary",), vmem_limit_bytes=VMEM_LIMIT),
        name="rmsnorm",
    )(x, gains)
    return outs


def _mm_kernel(a_ref, w_ref, x_ref, o_ref, *, w_is_nk, epilogue):
    w = w_ref[...].astype(BF16)
    if w_is_nk:
        acc = lax.dot_general(a_ref[...], w, _NT, preferred_element_type=F32)
    else:
        acc = jnp.dot(a_ref[...], w, preferred_element_type=F32)
    if epilogue == "residual":
        out = x_ref[...] + acc
    elif epilogue == "sigmoid":
        out = _sigmoid(acc)
    else:
        out = acc * x_ref[...]
    o_ref[...] = out.astype(o_ref.dtype)


def _matmul(a, w, *, out_dtype, scale=None, residual=None, sigmoid=False, n_cols=None, w_is_nk=False, name):
    m, k = a.shape
    n = (w.shape[0] if w_is_nk else w.shape[1]) if n_cols is None else n_cols
    tn = min(512, n)
    assert n % tn == 0
    tm = min(2048, m)
    a_spec = pl.BlockSpec((tm, k), lambda i, j: (i, 0), pipeline_mode=pl.Buffered(1))
    w_spec = pl.BlockSpec((tn, k), lambda i, j: (j, 0)) if w_is_nk else pl.BlockSpec((k, tn), lambda i, j: (0, j))
    o_spec = pl.BlockSpec((tm, tn), lambda i, j: (i, j))
    epilogue = "residual" if residual is not None else ("sigmoid" if sigmoid else "scale")
    kern = functools.partial(_mm_kernel, w_is_nk=w_is_nk, epilogue=epilogue)
    if residual is None:
        extra, extra_spec = scale, pl.BlockSpec((1, tn), lambda i, j: (0, j))
    else:
        extra, extra_spec = residual, o_spec
    return pl.pallas_call(
        kern,
        grid=(m // tm, n // tn),
        in_specs=[a_spec, w_spec, extra_spec],
        out_specs=o_spec,
        out_shape=jax.ShapeDtypeStruct((m, n), out_dtype),
        compiler_params=pltpu.CompilerParams(
            dimension_semantics=("arbitrary", "arbitrary"), vmem_limit_bytes=VMEM_LIMIT),
        name=name,
    )(a, w, extra)


def _cmp_kernel(r_ref, pa_ref, pb_ref, w1a_ref, w1b_ref, w2_ref, o_ref):
    r = r_ref[...].astype(F32)
    nr = r.shape[0]
    xa = (r + pa_ref[...]).astype(BF16)
    xb = (r + pb_ref[...]).astype(BF16)
    ya = jnp.dot(xa, w1a_ref[...], preferred_element_type=F32)
    yb = jnp.dot(xb, w1b_ref[...], preferred_element_type=F32)
    hid = ya + pltpu.roll(yb, nr - 1, 0)
    hid = hid * _sigmoid(hid)
    o_ref[...] = jnp.dot(hid.astype(BF16), w2_ref[...], preferred_element_type=F32).astype(o_ref.dtype)


def _compress(r, pa, pb, w1a, w1b, w2):
    two, bg, nr, kk = r.shape
    dh = w2.shape[-1]
    sq = pl.Squeezed()
    wspec = lambda shape: pl.BlockSpec((sq,) + shape, lambda t, i: (t, 0, 0))
    return pl.pallas_call(
        _cmp_kernel,
        grid=(two, bg),
        in_specs=[pl.BlockSpec((sq, sq, nr, kk), lambda t, i: (t, i, 0, 0)),
                  wspec((1, kk)), wspec((1, kk)), wspec((kk, dh)), wspec((kk, dh)), wspec((dh, dh))],
        out_specs=pl.BlockSpec((sq, sq, nr, dh), lambda t, i: (t, i, 0, 0)),
        out_shape=jax.ShapeDtypeStruct((two, bg, nr, dh), BF16),
        compiler_params=pltpu.CompilerParams(
            dimension_semantics=("arbitrary", "arbitrary"), vmem_limit_bytes=VMEM_LIMIT),
        name="nsa_compress",
    )(r, pa, pb, w1a, w1b, w2)


def _stack_heads(ref, n_heads, col0=0):
    return jnp.concatenate(
        [ref[:, col0 + h * HEAD_DIM:col0 + (h + 1) * HEAD_DIM] for h in range(n_heads)], axis=0)


def _query_in_tile(n_heads, tq, width):
    assert tq & (tq - 1) == 0, "query tile must be a power of two"
    return lax.broadcasted_iota(jnp.int32, (n_heads * tq, width), 0) & (tq - 1)


def _slope_extras(sl_ref, head0, n_heads, tq):
    lane = lax.broadcasted_iota(jnp.int32, (tq, LANES), 1)
    blocks = []
    for h in range(n_heads):
        x = jnp.zeros((tq, LANES), F32)
        for piece in range(3):
            hit = (lane == POS_COL + piece) | (lane == POS_COL + 3 + piece)
            x = jnp.where(hit, sl_ref[piece, head0 + h], x)
        blocks.append(x)
    return jnp.concatenate(blocks, axis=0)


def _causal_add(n_heads, tq, chunk, strict_future):
    qq = _query_in_tile(n_heads, tq, LANES)
    kk = lax.broadcasted_iota(jnp.int32, (n_heads * tq, LANES), 1) + chunk * LANES
    keep = (kk > qq) if strict_future else (kk <= qq)
    return jnp.where(keep, 0.0, NEG_INF)


def _positions_only(kx_tile):
    lane = lax.broadcasted_iota(jnp.int32, kx_tile.shape, 1)
    return jnp.where(lane >= POS_COL, kx_tile, jnp.zeros_like(kx_tile))


def _pad_rows_to_lanes(x):
    n = x.shape[0]
    assert n <= POS_COL, "block one-hot columns must stay clear of the slope/position columns"
    return jnp.concatenate([x, jnp.zeros((LANES - n, x.shape[1]), x.dtype)], axis=0)


def _online_tile(s_ref, p_ref, m_ref, l_ref, a_ref, acc_ref, v, *, rows, width, first):
    nch = width // LANES
    mt = s_ref[:, 0:LANES]
    for c in range(1, nch):
        mt = jnp.maximum(mt, s_ref[:, c * LANES:(c + 1) * LANES])
    mrow = jnp.max(mt, axis=-1, keepdims=True)
    if first:
        m_new = jnp.broadcast_to(mrow, (rows, LANES))
    else:
        m_old = m_ref[...]
        m_new = jnp.maximum(m_old, mrow)
        a_ref[...] = jnp.exp2(m_old - m_new)
    m_ref[...] = m_new
    ps = None
    for c in range(nch):
        cs = slice(c * LANES, (c + 1) * LANES)
        p = jnp.exp2(s_ref[:, cs] - m_new)
        p_ref[:, cs] = p.astype(BF16)
        ps = p if ps is None else ps + p
    if first:
        l_ref[...] = ps
    else:
        l_ref[...] = a_ref[...] * l_ref[...] + ps
    pv = jnp.dot(p_ref[:, 0:width], v, preferred_element_type=F32)
    if first:
        acc_ref[...] = pv
    else:
        acc_ref[...] = a_ref[...] * acc_ref[...] + pv


def _finish(l_ref, acc_ref):
    return acc_ref[...] * (1.0 / jnp.sum(l_ref[...], axis=-1, keepdims=True))


def _key_extras(pos, block=None):
    pos = pos[:, None]
    col = jnp.arange(LANES)[None, :]
    x = jnp.zeros((pos.shape[0], LANES), F32)
    if block is not None:
        x = jnp.where(col == pos // block, 1.0, x)
    x = jnp.where((col >= POS_COL) & (col < POS_COL + 3), (pos // POS_SPLIT) * POS_SPLIT, x)
    x = jnp.where((col >= POS_COL + 3) & (col < POS_COL + 6), pos % POS_SPLIT, x)
    return x.astype(BF16)


def _slope_pieces():
    s = 2.0 ** (-8.0 * jnp.arange(1, N_HEADS + 1, dtype=F32) / N_HEADS) * LOG2E
    a = s.astype(BF16).astype(F32)
    b = (s - a).astype(BF16).astype(F32)
    c = (s - a - b).astype(BF16).astype(F32)
    return jnp.stack([a, b, c])


def _nsa_kernel(sl_ref, q_ref, *refs, tq, tk, n_top, gps):
    z_refs = [refs[br * gps:(br + 1) * gps] for br in range(3)]
    (gt_ref, ks_ref, vs_ref, kw_ref, vw_ref, kc_ref, vc_ref, kx_ref, wx_ref, cx_ref, ovl_ref, o_ref,
     s_ref, p_ref, m_ref, l_ref, a_ref, acc_ref, ob_ref, used_ref) = refs[3 * gps:]
    gp = pl.program_id(1)
    i = pl.program_id(2)
    t0 = i * tq
    hpg = NSA_HPG
    gw = hpg * HEAD_DIM
    rows = hpg * tq
    grp = range(gps)
    kcol = lambda u: slice(u * HEAD_DIM, (u + 1) * HEAD_DIM)
    stats = lambda u: (s_ref.at[u], p_ref.at[u], m_ref.at[u], l_ref.at[u], a_ref.at[u], acc_ref.at[u])
    nqc = tq // LANES
    d0 = pl.multiple_of(t0, tq)
    qa = [_stack_heads(q_ref, hpg, u * gw) for u in grp]
    qx0 = [_slope_extras(sl_ref, (gp * gps + u) * hpg, hpg, tq).astype(BF16) for u in grp]
    q_aug0 = [jnp.concatenate([qa[u], qx0[u]], axis=1) for u in grp]

    nc = kc_ref.shape[1]
    cx = cx_ref[...]
    cend = lax.broadcasted_iota(jnp.int32, (1, nc), 1) * NSA_CMP_STRIDE + (NSA_CMP_LEN - 1)
    tqv = t0 + (lax.broadcasted_iota(jnp.int32, (rows, 1), 0) & (tq - 1))
    mask = cend <= tqv
    ovl = ovl_ref[...]
    nb = ovl.shape[0]
    j = lax.broadcasted_iota(jnp.int32, (nb, tq), 0)
    blkq = (t0 + lax.broadcasted_iota(jnp.int32, (nb, tq), 1)) >> (NSA_SEL_BLOCK.bit_length() - 1)
    forced = (j == 0) | (j == blkq) | (j == blkq - 1)
    blk0 = t0 >> (NSA_SEL_BLOCK.bit_length() - 1)
    q_aug = []
    any_sel = None
    for u in grp:
        kc_aug = jnp.concatenate([kc_ref[u], cx], axis=1)
        s = lax.dot_general(q_aug0[u], kc_aug, _NT, preferred_element_type=F32)
        s = jnp.where(mask, s, NEG_INF)
        mx = jnp.max(s, axis=-1, keepdims=True)
        e = jnp.where(mask, jnp.exp2(s - mx), 0.0)
        den = jnp.sum(e, axis=-1, keepdims=True)
        p = e * (1.0 / jnp.where(den > 0, den, 1.0))
        ob_ref[u, 0] = jnp.dot(p.astype(BF16), vc_ref[u], preferred_element_type=F32)
        psum = p[0:tq]
        for h in range(1, hpg):
            psum = psum + p[h * tq:(h + 1) * tq]
        p_hi = psum.astype(BF16)
        p_lo = (psum - p_hi.astype(F32)).astype(BF16)
        imp = (lax.dot_general(ovl, p_hi, _NT, preferred_element_type=F32)
               + lax.dot_general(ovl, p_lo, _NT, preferred_element_type=F32))
        imp = jnp.where(forced, FORCE_SCORE, jnp.where(j > blkq, NEG_INF, imp))
        rank = jnp.zeros((nb, tq), jnp.int32)
        for ii in range(nb):
            row = imp[ii:ii + 1, :]
            beats = (row > imp) | ((row == imp) & (j > ii))
            rank = rank + beats.astype(jnp.int32)
        selneg = jnp.where((rank < n_top) & (j < blk0), 0.0, NEG_INF).astype(F32)
        any_sel = selneg if any_sel is None else jnp.maximum(any_sel, selneg)
        selneg_q = _pad_rows_to_lanes(selneg).T.astype(BF16)
        qx = qx0[u] + jnp.concatenate([selneg_q] * hpg, axis=0)
        q_aug.append(jnp.concatenate([qa[u], qx], axis=1))
    bpt = tk // NSA_SEL_BLOCK
    for t in range(nb // bpt):
        used_ref[t] = (jnp.max(any_sel[t * bpt:(t + 1) * bpt, :]) > 0.5 * NEG_INF).astype(jnp.int32)

    causal_add = [_causal_add(hpg, tq, c, False) for c in range(nqc)]
    future_add = [_causal_add(hpg, tq, c, True) for c in range(nqc)]
    wlen = NSA_WINDOW + tq
    wxt = wx_ref[pl.ds(d0, wlen), :]
    for u in grp:
        su = s_ref.at[u]
        kw_aug = jnp.concatenate([kw_ref[pl.ds(d0, wlen), kcol(u)], wxt], axis=1)
        su[:, 0:wlen] = lax.dot_general(q_aug0[u], kw_aug, _NT, preferred_element_type=F32)
        for c in range(nqc):
            lo = slice(c * LANES, (c + 1) * LANES)
            su[:, lo] = su[:, lo] + future_add[c]
            hi = slice(NSA_WINDOW + c * LANES, NSA_WINDOW + (c + 1) * LANES)
            su[:, hi] = su[:, hi] + causal_add[c]
    for u in grp:
        _online_tile(*stats(u), vw_ref[pl.ds(d0, wlen), kcol(u)], rows=rows, width=wlen, first=True)
        ob_ref[u, 2] = _finish(l_ref.at[u], acc_ref.at[u])

    kxd = _positions_only(kx_ref[pl.ds(d0, tq), :])
    for u in grp:
        su = s_ref.at[u]
        kd_aug = jnp.concatenate([ks_ref[pl.ds(d0, tq), kcol(u)], kxd], axis=1)
        su[:, 0:tq] = lax.dot_general(q_aug0[u], kd_aug, _NT, preferred_element_type=F32)
        for c in range(nqc):
            cs = slice(c * LANES, (c + 1) * LANES)
            su[:, cs] = su[:, cs] + causal_add[c]
    for u in grp:
        _online_tile(*stats(u), vs_ref[pl.ds(d0, tq), kcol(u)], rows=rows, width=tq, first=True)

    def sel_body(kt, carry):
        @pl.when(used_ref[kt] != 0)
        def _():
            k0 = pl.multiple_of(kt * tk, tk)
            kxt = kx_ref[pl.ds(k0, tk), :]
            for u in grp:
                k_aug = jnp.concatenate([ks_ref[pl.ds(k0, tk), kcol(u)], kxt], axis=1)
                s_ref[u, :, 0:tk] = lax.dot_general(q_aug[u], k_aug, _NT, preferred_element_type=F32)
            for u in grp:
                _online_tile(*stats(u), vs_ref[pl.ds(k0, tk), kcol(u)], rows=rows, width=tk, first=False)
        return carry

    lax.fori_loop(0, (t0 + tk - 1) // tk, sel_body, 0)
    for u in grp:
        ob_ref[u, 1] = _finish(l_ref.at[u], acc_ref.at[u])

    gates = gt_ref[...].astype(F32)
    for u in grp:
        gts = pltpu.roll(gates, (LANES - (gp * gps + u) * hpg) % LANES, 1)
        for h in range(hpg):
            rs = slice(h * tq, (h + 1) * tq)
            cs = slice(h * HEAD_DIM, (h + 1) * HEAD_DIM)
            mix = jnp.zeros((tq, HEAD_DIM), F32)
            for br in range(3):
                z = z_refs[br][u][:, cs].astype(F32)
                gate = gts[:, br * N_HEADS + h:br * N_HEADS + h + 1]
                mix = mix + gate * ob_ref[u, br, rs, :] * (z * _sigmoid(z))
            o_ref[:, u * gw + h * HEAD_DIM:u * gw + (h + 1) * HEAD_DIM] = mix.astype(o_ref.dtype)


def _nsa_attention(pieces, proj, gate_logits, win, kvc, ovl_t, *, batch, seq):
    m = proj.shape[0]
    tq = min(128, seq)
    tk = min(512, seq)
    nt = seq // tq
    nb = seq // NSA_SEL_BLOCK
    nc = seq // NSA_CMP_STRIDE
    hpg, g_n, dh = NSA_HPG, NSA_KV_HEADS, HEAD_DIM
    gps = NSA_GROUPS_PER_STEP
    ngp = g_n // gps
    gw = hpg * dh
    q_cols = N_HEADS * dh
    kv_col0 = q_cols // (gps * dh)
    z_col0 = (q_cols + 3 * 2 * g_n * dh) // gw
    rows = hpg * tq
    wlen = NSA_WINDOW + tq
    sq = pl.Squeezed()
    assert nb <= POS_COL and g_n % gps == 0
    kx = _key_extras(jnp.arange(seq), NSA_SEL_BLOCK)
    wx = _key_extras(jnp.arange(seq + NSA_WINDOW))
    cx = _key_extras(jnp.arange(nc) * NSA_CMP_STRIDE + (NSA_CMP_LEN - 1))

    def kv_spec(branch, kv):
        c = kv_col0 + (branch * 2 + kv) * ngp
        return pl.BlockSpec((seq, gps * dh), lambda b, g, i: (b, c + g))

    def z_spec(branch, u):
        c = z_col0 + branch * g_n + u
        return pl.BlockSpec((tq, gw), lambda b, g, i: (b * nt + i, c + g * gps))

    win_spec = lambda kv: pl.BlockSpec((sq, seq + NSA_WINDOW, gps * dh), lambda b, g, i: (b, 0, kv * ngp + g))
    cmp_spec = lambda kv: pl.BlockSpec((sq, gps, nc, dh), lambda b, g, i: (kv, b * ngp + g, 0, 0))
    full = lambda arr: pl.BlockSpec(arr.shape, lambda b, g, i: (0,) * arr.ndim)
    in_specs = (
        [pl.BlockSpec(memory_space=pltpu.SMEM),
         pl.BlockSpec((tq, gps * gw), lambda b, g, i: (b * nt + i, g))]
        + [z_spec(br, u) for br in range(3) for u in range(gps)]
        + [pl.BlockSpec((tq, LANES), lambda b, g, i: (b * nt + i, 0)),
           kv_spec(1, 0), kv_spec(1, 1),
           win_spec(0), win_spec(1),
           cmp_spec(0), cmp_spec(1),
           full(kx), full(wx), full(cx), full(ovl_t)])
    kern = functools.partial(_nsa_kernel, tq=tq, tk=tk, n_top=min(NSA_SEL_TOPN, nb), gps=gps)
    return pl.pallas_call(
        kern,
        grid=(batch, ngp, nt),
        in_specs=in_specs,
        out_specs=pl.BlockSpec((tq, gps * gw), lambda b, g, i: (b * nt + i, g)),
        out_shape=jax.ShapeDtypeStruct((m, q_cols), BF16),
        scratch_shapes=[pltpu.VMEM((gps, rows, max(wlen, tk)), F32),
                        pltpu.VMEM((gps, rows, max(wlen, tk)), BF16),
                        pltpu.VMEM((gps, rows, LANES), F32),
                        pltpu.VMEM((gps, rows, LANES), F32),
                        pltpu.VMEM((gps, rows, LANES), F32),
                        pltpu.VMEM((gps, rows, dh), F32),
                        pltpu.VMEM((gps, 3, rows, dh), F32),
                        pltpu.SMEM((seq // tk,), jnp.int32)],
        compiler_params=pltpu.CompilerParams(
            dimension_semantics=("arbitrary", "arbitrary", "arbitrary"), vmem_limit_bytes=VMEM_LIMIT),
        name="nsa_attention",
    )(pieces, proj, *([proj] * (3 * gps)), gate_logits, proj, proj, win, win, kvc, kvc, kx, wx, cx, ovl_t)


def _moba_kernel(sl_ref, q_ref, z_ref, k_ref, v_ref, mx_ref, o_ref,
                 km_ref, s_ref, p_ref, m_ref, l_ref, a_ref, acc_ref, *, tq, tk, n_top, gps):
    gp = pl.program_id(1)
    i = pl.program_id(2)
    hpg = MOBA_HPG
    gw = hpg * HEAD_DIM
    rows = hpg * tq
    seq = k_ref.shape[0]
    nbm = seq // MOBA_BLOCK
    bpt = tk // MOBA_BLOCK
    grp = range(gps)
    kcol = lambda u: slice(u * HEAD_DIM, (u + 1) * HEAD_DIM)
    stats = lambda u: (s_ref.at[u], p_ref.at[u], m_ref.at[u], l_ref.at[u], a_ref.at[u], acc_ref.at[u])

    @pl.when(i == 0)
    def _():
        blk = lax.broadcasted_iota(jnp.int32, (nbm, seq), 1) >> (MOBA_BLOCK.bit_length() - 1)
        avg = jnp.where(blk == lax.broadcasted_iota(jnp.int32, (nbm, seq), 0), 1.0 / MOBA_BLOCK, 0.0)
        for u in grp:
            km_ref[u] = jnp.dot(avg.astype(BF16), k_ref[:, kcol(u)], preferred_element_type=F32)

    n_io = lax.broadcasted_iota(jnp.int32, (nbm, rows), 0)
    past = n_io < i
    q_aug0, q_aug = [], []
    for u in grp:
        qa = _stack_heads(q_ref, hpg, u * gw)
        qx0 = _slope_extras(sl_ref, (gp * gps + u) * hpg, hpg, tq).astype(BF16)
        km = km_ref[u]
        km_hi = km.astype(BF16)
        km_lo = (km - km_hi.astype(F32)).astype(BF16)
        sb = (lax.dot_general(km_hi, qa, _NT, preferred_element_type=F32)
              + lax.dot_general(km_lo, qa, _NT, preferred_element_type=F32))
        sb = jnp.where(past, sb, NEG_INF)
        rank = jnp.zeros((nbm, rows), jnp.int32)
        for mm in range(nbm):
            row = sb[mm:mm + 1, :]
            beats = (row > sb) | ((row == sb) & (n_io > mm))
            rank = rank + beats.astype(jnp.int32)
        selneg = jnp.where(past & (rank < n_top), 0.0, NEG_INF).astype(F32)
        qx = qx0 + _pad_rows_to_lanes(selneg).T.astype(BF16)
        q_aug0.append(jnp.concatenate([qa, qx0], axis=1))
        q_aug.append(jnp.concatenate([qa, qx], axis=1))

    d0 = pl.multiple_of(i * tq, tq)
    mxd = _positions_only(mx_ref[pl.ds(d0, tq), :])
    causal_add = [_causal_add(hpg, tq, c, False) for c in range(tq // LANES)]
    for u in grp:
        su = s_ref.at[u]
        kd_aug = jnp.concatenate([k_ref[pl.ds(d0, tq), kcol(u)], mxd], axis=1)
        su[:, 0:tq] = lax.dot_general(q_aug0[u], kd_aug, _NT, preferred_element_type=F32)
        for c in range(tq // LANES):
            cs = slice(c * LANES, (c + 1) * LANES)
            su[:, cs] = su[:, cs] + causal_add[c]
    for u in grp:
        _online_tile(*stats(u), v_ref[pl.ds(d0, tq), kcol(u)], rows=rows, width=tq, first=True)

    def body(t, carry):
        k0 = pl.multiple_of(t * tk, tk)
        mxt = mx_ref[pl.ds(k0, tk), :]
        for u in grp:
            k_aug = jnp.concatenate([k_ref[pl.ds(k0, tk), kcol(u)], mxt], axis=1)
            s_ref[u] = lax.dot_general(q_aug[u], k_aug, _NT, preferred_element_type=F32)
        for u in grp:
            _online_tile(*stats(u), v_ref[pl.ds(k0, tk), kcol(u)], rows=rows, width=tk, first=False)
        return carry

    lax.fori_loop(0, (i + bpt - 1) // bpt, body, 0)
    for u in grp:
        o = _finish(l_ref.at[u], acc_ref.at[u])
        for h in range(hpg):
            cs = slice(u * gw + h * HEAD_DIM, u * gw + (h + 1) * HEAD_DIM)
            z = z_ref[:, cs].astype(F32)
            o_ref[:, cs] = (o[h * tq:(h + 1) * tq] * (z * _sigmoid(z))).astype(o_ref.dtype)


def _moba_attention(pieces, qz, kv, *, batch, seq):
    m = qz.shape[0]
    tq = MOBA_BLOCK
    tk = min(2 * MOBA_BLOCK, seq)
    nt = seq // tq
    hpg, g_n, dh = MOBA_HPG, MOBA_KV_HEADS, HEAD_DIM
    gps = MOBA_GROUPS_PER_STEP
    ngp = g_n // gps
    gw = gps * hpg * dh
    rows = hpg * tq
    nbm = seq // MOBA_BLOCK
    assert nbm <= POS_COL and g_n % gps == 0
    mx = _key_extras(jnp.arange(seq), MOBA_BLOCK)
    in_specs = [
        pl.BlockSpec(memory_space=pltpu.SMEM),
        pl.BlockSpec((tq, gw), lambda b, g, i: (b * nt + i, g)),
        pl.BlockSpec((tq, gw), lambda b, g, i: (b * nt + i, ngp + g)),
        pl.BlockSpec((seq, gps * dh), lambda b, g, i: (b, g)),
        pl.BlockSpec((seq, gps * dh), lambda b, g, i: (b, ngp + g)),
        pl.BlockSpec(mx.shape, lambda b, g, i: (0, 0)),
    ]
    kern = functools.partial(_moba_kernel, tq=tq, tk=tk, n_top=min(MOBA_TOPK, nbm), gps=gps)
    return pl.pallas_call(
        kern,
        grid=(batch, ngp, nt),
        in_specs=in_specs,
        out_specs=pl.BlockSpec((tq, gw), lambda b, g, i: (b * nt + i, g)),
        out_shape=jax.ShapeDtypeStruct((m, N_HEADS * dh), BF16),
        scratch_shapes=[pltpu.VMEM((gps, nbm, dh), F32),
                        pltpu.VMEM((gps, rows, tk), F32),
                        pltpu.VMEM((gps, rows, tk), BF16),
                        pltpu.VMEM((gps, rows, LANES), F32),
                        pltpu.VMEM((gps, rows, LANES), F32),
                        pltpu.VMEM((gps, rows, LANES), F32),
                        pltpu.VMEM((gps, rows, dh), F32)],
        compiler_params=pltpu.CompilerParams(
            dimension_semantics=("arbitrary", "arbitrary", "arbitrary"), vmem_limit_bytes=VMEM_LIMIT),
        name="moba_attention",
    )(pieces, qz, qz, kv, kv, mx)


def _nsa_layer(h, norm_g, w_in, pos_k, pos_v, w1_k, w2_k, w1_v, w2_v, w_out, *, batch, seq):
    dh, g_n = HEAD_DIM, NSA_KV_HEADS
    q_cols = N_HEADS * dh
    kv_cols = 3 * 2 * g_n * dh
    n_main = q_cols + kv_cols + 3 * q_cols
    n_gate = w_in.shape[1] - n_main
    assert n_gate == 3 * N_HEADS <= LANES and n_main % LANES == 0
    (xn,) = _rmsnorm(h, norm_g[None, :], BF16)
    scale = jnp.where(jnp.arange(n_main) < q_cols, dh ** -0.5 * LOG2E, 1.0).astype(F32)[None, :]
    w_t = w_in.T
    proj = _matmul(xn, w_t, out_dtype=BF16, scale=scale, n_cols=n_main, w_is_nk=True,
                   name="nsa_in_proj")
    w_gate_t = jnp.pad(w_t[n_main:], ((0, LANES - n_gate), (0, 0)))
    gates = _matmul(xn, w_gate_t, out_dtype=BF16, scale=jnp.ones((1, LANES), F32), sigmoid=True,
                    w_is_nk=True, name="nsa_gate_proj")

    half = NSA_CMP_STRIDE
    nr = seq // half
    raw = proj[:, q_cols:q_cols + 2 * g_n * dh].reshape(batch, nr, half, 2, g_n, dh)
    r = raw.transpose(3, 0, 4, 1, 2, 5).reshape(2, batch * g_n, nr, half * dh)
    pos = jnp.stack([pos_k, pos_v])
    pa = pos[:, :half].reshape(2, 1, half * dh)
    pb = pos[:, half:].reshape(2, 1, half * dh)
    w1 = jnp.stack([w1_k, w1_v])
    w1a = w1[:, :half].reshape(2, half * dh, dh).astype(BF16)
    w1b = w1[:, half:].reshape(2, half * dh, dh).astype(BF16)
    w2 = jnp.stack([w2_k, w2_v]).astype(BF16)
    kvc = _compress(r, pa, pb, w1a, w1b, w2)

    wcol = q_cols + 2 * 2 * g_n * dh
    win = proj[:, wcol:wcol + 2 * g_n * dh].reshape(batch, seq, 2 * g_n * dh)
    win = jnp.pad(win, ((0, 0), (NSA_WINDOW, 0), (0, 0)))

    nb = seq // NSA_SEL_BLOCK
    cstart = jnp.arange(nr)[None, :] * NSA_CMP_STRIDE
    sstart = jnp.arange(nb)[:, None] * NSA_SEL_BLOCK
    ovl_t = ((cstart <= sstart + NSA_SEL_BLOCK - 1)
             & (cstart + NSA_CMP_LEN - 1 >= sstart)
             & (jnp.arange(nr)[None, :] < (seq - NSA_CMP_LEN) // NSA_CMP_STRIDE + 1)).astype(BF16)
    mix = _nsa_attention(_slope_pieces(), proj, gates, win, kvc, ovl_t, batch=batch, seq=seq)
    return _matmul(mix, w_out, out_dtype=F32, residual=h, name="nsa_out_proj")


def _moba_shared_kv(xn_kv, kv_w):
    one = jnp.ones((1, kv_w.shape[1]), F32)
    return _matmul(xn_kv, kv_w, out_dtype=BF16, scale=one, name="moba_kv_proj")


def _moba_layer(h, xn_q, kv, w_in, w_out, *, batch, seq):
    dh = HEAD_DIM
    q_cols = N_HEADS * dh
    scale = jnp.where(jnp.arange(w_in.shape[1]) < q_cols, dh ** -0.5 * LOG2E, 1.0).astype(F32)[None, :]
    qz = _matmul(xn_q, w_in, out_dtype=BF16, scale=scale, name="moba_in_proj")
    o = _moba_attention(_slope_pieces(), qz, kv, batch=batch, seq=seq)
    return _matmul(o, w_out, out_dtype=F32, residual=h, name="moba_out_proj")


def kernel(x, a_norm_g, a_w_in, a_cmp_pos_k, a_cmp_pos_v, a_cmp_w1_k, a_cmp_w2_k, a_cmp_w1_v, a_cmp_w2_v,
           a_w_out, kv_norm_g, kv_w, b_norm_g, b_w_in, b_w_out, final_norm_g):
    batch, seq, d = x.shape
    h = x.reshape(batch * seq, d)
    for layer in range(a_norm_g.shape[0]):
        h = _nsa_layer(h, a_norm_g[layer], a_w_in[layer], a_cmp_pos_k[layer], a_cmp_pos_v[layer],
                       a_cmp_w1_k[layer], a_cmp_w2_k[layer], a_cmp_w1_v[layer], a_cmp_w2_v[layer],
                       a_w_out[layer], batch=batch, seq=seq)
    kv = None
    for layer in range(b_norm_g.shape[0]):
        if layer == 0:
            xn_kv, xn_q = _rmsnorm(h, jnp.stack([kv_norm_g, b_norm_g[layer]]), BF16)
            kv = _moba_shared_kv(xn_kv, kv_w)
        else:
            (xn_q,) = _rmsnorm(h, b_norm_g[layer][None, :], BF16)
        h = _moba_layer(h, xn_q, kv, b_w_in[layer], b_w_out[layer], batch=batch, seq=seq)
    (out,) = _rmsnorm(h, final_norm_g[None, :], F32)
    return out.reshape(batch, seq, d)
```

```python
import functools

import jax
import jax.numpy as jnp
from jax import lax
from jax.experimental import pallas as pl
from jax.experimental.pallas import tpu as pltpu

F32 = jnp.float32
BF16 = jnp.bfloat16

N_HEADS = 32
HEAD_DIM = 128
NSA_KV_HEADS = 4
NSA_HPG = N_HEADS // NSA_KV_HEADS
NSA_CMP_LEN = 32
NSA_CMP_STRIDE = 16
NSA_SEL_BLOCK = 64
NSA_SEL_TOPN = 16
NSA_WINDOW = 512
MOBA_KV_HEADS = 8
MOBA_HPG = N_HEADS // MOBA_KV_HEADS
MOBA_BLOCK = 256
MOBA_TOPK = 3
RMS_EPS = 1e-6
NEG_INF = -1e30
FORCE_SCORE = 1e9
LOG2E = 1.4426950408889634

LANES = 128
VMEM_LIMIT = 56 * 1024 * 1024
NSA_GROUPS_PER_STEP = 2
MOBA_GROUPS_PER_STEP = 4
POS_COL = 120
POS_SPLIT = 64

_NT = (((1,), (1,)), ((), ()))


def _sigmoid(x):
    return 1.0 / (1.0 + jnp.exp(-x))


def _rmsnorm_kernel(x_ref, g_ref, *o_refs):
    x = x_ref[...].astype(F32)
    y = x * lax.rsqrt(jnp.mean(x * x, axis=-1, keepdims=True) + RMS_EPS)
    for i, o_ref in enumerate(o_refs):
        o_ref[...] = (y * g_ref[i:i + 1, :]).astype(o_ref.dtype)


def _rmsnorm(x, gains, out_dtype):
    m, d = x.shape
    n = gains.shape[0]
    tm = min(256, m)
    outs = pl.pallas_call(
        _rmsnorm_kernel,
        grid=(m // tm,),
        in_specs=[pl.BlockSpec((tm, d), lambda i: (i, 0)),
                  pl.BlockSpec((n, d), lambda i: (0, 0))],
        out_specs=[pl.BlockSpec((tm, d), lambda i: (i, 0)) for _ in range(n)],
        out_shape=[jax.ShapeDtypeStruct((m, d), out_dtype) for _ in range(n)],
        compiler_params=pltpu.CompilerParams(
            dimension_semantics=("arbitrary",), vmem_limit_bytes=VMEM_LIMIT),
        name="rmsnorm",
    )(x, gains)
    return outs


def _mm_kernel(a_ref, w_ref, x_ref, o_ref, *, w_is_nk, epilogue):
    w = w_ref[...].astype(BF16)
    if w_is_nk:
        acc = lax.dot_general(a_ref[...], w, _NT, preferred_element_type=F32)
    else:
        acc = jnp.dot(a_ref[...], w, preferred_element_type=F32)
    if epilogue == "residual":
        out = x_ref[...] + acc
    elif epilogue == "sigmoid":
        out = _sigmoid(acc)
    else:
        out = acc * x_ref[...]
    o_ref[...] = out.astype(o_ref.dtype)


def _matmul(a, w, *, out_dtype, scale=None, residual=None, sigmoid=False, n_cols=None, w_is_nk=False, name):
    m, k = a.shape
    n = (w.shape[0] if w_is_nk else w.shape[1]) if n_cols is None else n_cols
    tn = min(512, n)
    assert n % tn == 0
    if n // tn >= 32:
        tm = min(2048, m)
        a_spec = pl.BlockSpec((tm, k), lambda i, j: (i, 0), pipeline_mode=pl.Buffered(1))
    else:
        tm = min(1024, m)
        a_spec = pl.BlockSpec((tm, k), lambda i, j: (i, 0))
    w_spec = pl.BlockSpec((tn, k), lambda i, j: (j, 0)) if w_is_nk else pl.BlockSpec((k, tn), lambda i, j: (0, j))
    o_spec = pl.BlockSpec((tm, tn), lambda i, j: (i, j))
    epilogue = "residual" if residual is not None else ("sigmoid" if sigmoid else "scale")
    kern = functools.partial(_mm_kernel, w_is_nk=w_is_nk, epilogue=epilogue)
    if residual is None:
        extra, extra_spec = scale, pl.BlockSpec((1, tn), lambda i, j: (0, j))
    else:
        extra, extra_spec = residual, o_spec
    return pl.pallas_call(
        kern,
        grid=(m // tm, n // tn),
        in_specs=[a_spec, w_spec, extra_spec],
        out_specs=o_spec,
        out_shape=jax.ShapeDtypeStruct((m, n), out_dtype),
        compiler_params=pltpu.CompilerParams(
            dimension_semantics=("arbitrary", "arbitrary"), vmem_limit_bytes=VMEM_LIMIT),
        name=name,
    )(a, w, extra)


def _cmp_kernel(raw_ref, pa_ref, pb_ref, w1a_ref, w1b_ref, w2_ref, o_ref, raw32_ref):
    half = NSA_CMP_STRIDE
    nr = raw_ref.shape[0] // half
    raw32_ref[...] = raw_ref[...].astype(F32)
    r = jnp.concatenate([raw32_ref[pl.ds(l, nr, stride=half), :] for l in range(half)], axis=1)
    xa = (r + pa_ref[...]).astype(BF16)
    xb = (r + pb_ref[...]).astype(BF16)
    ya = jnp.dot(xa, w1a_ref[...], preferred_element_type=F32)
    yb = jnp.dot(xb, w1b_ref[...], preferred_element_type=F32)
    hid = ya + pltpu.roll(yb, nr - 1, 0)
    hid = hid * _sigmoid(hid)
    o_ref[...] = jnp.dot(hid.astype(BF16), w2_ref[...], preferred_element_type=F32).astype(o_ref.dtype)


def _compress(proj, col0, pa, pb, w1a, w1b, w2, *, batch, seq):
    g_n, dh = NSA_KV_HEADS, HEAD_DIM
    nr = seq // NSA_CMP_STRIDE
    kk = NSA_CMP_STRIDE * dh
    cb = col0 // dh
    sq = pl.Squeezed()
    wspec = lambda shape: pl.BlockSpec((sq,) + shape, lambda t, i: (t, 0, 0))
    return pl.pallas_call(
        _cmp_kernel,
        grid=(2, batch * g_n),
        in_specs=[pl.BlockSpec((seq, dh), lambda t, i: (i // g_n, cb + t * g_n + i % g_n)),
                  wspec((1, kk)), wspec((1, kk)), wspec((kk, dh)), wspec((kk, dh)), wspec((dh, dh))],
        out_specs=pl.BlockSpec((sq, sq, nr, dh), lambda t, i: (t, i, 0, 0)),
        out_shape=jax.ShapeDtypeStruct((2, batch * g_n, nr, dh), BF16),
        scratch_shapes=[pltpu.VMEM((seq, dh), F32)],
        compiler_params=pltpu.CompilerParams(
            dimension_semantics=("arbitrary", "arbitrary"), vmem_limit_bytes=VMEM_LIMIT),
        name="nsa_compress",
    )(proj, pa, pb, w1a, w1b, w2)


def _stack_heads(ref, n_heads, col0=0):
    return jnp.concatenate(
        [ref[:, col0 + h * HEAD_DIM:col0 + (h + 1) * HEAD_DIM] for h in range(n_heads)], axis=0)


def _query_in_tile(n_heads, tq, width):
    assert tq & (tq - 1) == 0, "query tile must be a power of two"
    return lax.broadcasted_iota(jnp.int32, (n_heads * tq, width), 0) & (tq - 1)


def _slope_extras(sl_ref, head0, n_heads, tq):
    lane = lax.broadcasted_iota(jnp.int32, (tq, LANES), 1)
    blocks = []
    for h in range(n_heads):
        x = jnp.zeros((tq, LANES), F32)
        for piece in range(3):
            hit = (lane == POS_COL + piece) | (lane == POS_COL + 3 + piece)
            x = jnp.where(hit, sl_ref[piece, head0 + h], x)
        blocks.append(x)
    return jnp.concatenate(blocks, axis=0)


def _causal_add(n_heads, tq, chunk, strict_future):
    qq = _query_in_tile(n_heads, tq, LANES)
    kk = lax.broadcasted_iota(jnp.int32, (n_heads * tq, LANES), 1) + chunk * LANES
    keep = (kk > qq) if strict_future else (kk <= qq)
    return jnp.where(keep, 0.0, NEG_INF)


def _positions_only(kx_tile):
    lane = lax.broadcasted_iota(jnp.int32, kx_tile.shape, 1)
    return jnp.where(lane >= POS_COL, kx_tile, jnp.zeros_like(kx_tile))


def _pad_rows_to_lanes(x):
    n = x.shape[0]
    assert n <= POS_COL, "block one-hot columns must stay clear of the slope/position columns"
    return jnp.concatenate([x, jnp.zeros((LANES - n, x.shape[1]), x.dtype)], axis=0)


def _online_tile(s_ref, p_ref, m_ref, l_ref, a_ref, acc_ref, v, *, rows, width, first):
    nch = width // LANES
    mt = s_ref[:, 0:LANES]
    for c in range(1, nch):
        mt = jnp.maximum(mt, s_ref[:, c * LANES:(c + 1) * LANES])
    mrow = jnp.max(mt, axis=-1, keepdims=True)
    if first:
        m_new = jnp.broadcast_to(mrow, (rows, LANES))
    else:
        m_old = m_ref[...]
        m_new = jnp.maximum(m_old, mrow)
        a_ref[...] = jnp.exp2(m_old - m_new)
    m_ref[...] = m_new
    ps = None
    for c in range(nch):
        cs = slice(c * LANES, (c + 1) * LANES)
        p = jnp.exp2(s_ref[:, cs] - m_new)
        p_ref[:, cs] = p.astype(BF16)
        ps = p if ps is None else ps + p
    if first:
        l_ref[...] = ps
    else:
        l_ref[...] = a_ref[...] * l_ref[...] + ps
    pv = jnp.dot(p_ref[:, 0:width], v, preferred_element_type=F32)
    if first:
        acc_ref[...] = pv
    else:
        acc_ref[...] = a_ref[...] * acc_ref[...] + pv


def _finish(l_ref, acc_ref):
    return acc_ref[...] * (1.0 / jnp.sum(l_ref[...], axis=-1, keepdims=True))


def _key_extras(pos, block=None):
    pos = pos[:, None]
    col = jnp.arange(LANES)[None, :]
    x = jnp.zeros((pos.shape[0], LANES), F32)
    if block is not None:
        x = jnp.where(col == pos // block, 1.0, x)
    x = jnp.where((col >= POS_COL) & (col < POS_COL + 3), (pos // POS_SPLIT) * POS_SPLIT, x)
    x = jnp.where((col >= POS_COL + 3) & (col < POS_COL + 6), pos % POS_SPLIT, x)
    return x.astype(BF16)


def _slope_pieces():
    s = 2.0 ** (-8.0 * jnp.arange(1, N_HEADS + 1, dtype=F32) / N_HEADS) * LOG2E
    a = s.astype(BF16).astype(F32)
    b = (s - a).astype(BF16).astype(F32)
    c = (s - a - b).astype(BF16).astype(F32)
    return jnp.stack([a, b, c])


def _nsa_kernel(sl_ref, q_ref, *refs, tq, tk, n_top, gps):
    z_refs = [refs[br * gps:(br + 1) * gps] for br in range(3)]
    (gt_ref, ks_ref, vs_ref, kw_ref, vw_ref, kc_ref, vc_ref, kx_ref, wx_ref, cx_ref, ovl_ref, o_ref,
     s_ref, p_ref, m_ref, l_ref, a_ref, acc_ref, ob_ref, used_ref) = refs[3 * gps:]
    gp = pl.program_id(1)
    i = pl.program_id(2)
    t0 = i * tq
    hpg = NSA_HPG
    gw = hpg * HEAD_DIM
    rows = hpg * tq
    grp = range(gps)
    kcol = lambda u: slice(u * HEAD_DIM, (u + 1) * HEAD_DIM)
    stats = lambda u: (s_ref.at[u], p_ref.at[u], m_ref.at[u], l_ref.at[u], a_ref.at[u], acc_ref.at[u])
    nqc = tq // LANES
    d0 = pl.multiple_of(t0, tq)
    qa = [_stack_heads(q_ref, hpg, u * gw) for u in grp]
    qx0 = [_slope_extras(sl_ref, (gp * gps + u) * hpg, hpg, tq).astype(BF16) for u in grp]
    q_aug0 = [jnp.concatenate([qa[u], qx0[u]], axis=1) for u in grp]

    nc = kc_ref.shape[1]
    cx = cx_ref[...]
    cend = lax.broadcasted_iota(jnp.int32, (1, nc), 1) * NSA_CMP_STRIDE + (NSA_CMP_LEN - 1)
    tqv = t0 + (lax.broadcasted_iota(jnp.int32, (rows, 1), 0) & (tq - 1))
    mask = cend <= tqv
    ovl = ovl_ref[...]
    nb = ovl.shape[0]
    j = lax.broadcasted_iota(jnp.int32, (nb, tq), 0)
    blkq = (t0 + lax.broadcasted_iota(jnp.int32, (nb, tq), 1)) >> (NSA_SEL_BLOCK.bit_length() - 1)
    forced = (j == 0) | (j == blkq) | (j == blkq - 1)
    blk0 = t0 >> (NSA_SEL_BLOCK.bit_length() - 1)
    q_aug = []
    any_sel = None
    for u in grp:
        kc_aug = jnp.concatenate([kc_ref[u], cx], axis=1)
        s = lax.dot_general(q_aug0[u], kc_aug, _NT, preferred_element_type=F32)
        s = jnp.where(mask, s, NEG_INF)
        mx = jnp.max(s, axis=-1, keepdims=True)
        e = jnp.where(mask, jnp.exp2(s - mx), 0.0)
        den = jnp.sum(e, axis=-1, keepdims=True)
        p = e * (1.0 / jnp.where(den > 0, den, 1.0))
        ob_ref[u, 0] = jnp.dot(p.astype(BF16), vc_ref[u], preferred_element_type=F32)
        psum = p[0:tq]
        for h in range(1, hpg):
            psum = psum + p[h * tq:(h + 1) * tq]
        p_hi = psum.astype(BF16)
        p_lo = (psum - p_hi.astype(F32)).astype(BF16)
        imp = (lax.dot_general(ovl, p_hi, _NT, preferred_element_type=F32)
               + lax.dot_general(ovl, p_lo, _NT, preferred_element_type=F32))
        imp = jnp.where(forced, FORCE_SCORE, jnp.where(j > blkq, NEG_INF, imp))
        rank = jnp.zeros((nb, tq), jnp.int32)
        for ii in range(nb):
            row = imp[ii:ii + 1, :]
            beats = (row > imp) | ((row == imp) & (j > ii))
            rank = rank + beats.astype(jnp.int32)
        selneg = jnp.where((rank < n_top) & (j < blk0), 0.0, NEG_INF).astype(F32)
        any_sel = selneg if any_sel is None else jnp.maximum(any_sel, selneg)
        selneg_q = _pad_rows_to_lanes(selneg).T.astype(BF16)
        qx = qx0[u] + jnp.concatenate([selneg_q] * hpg, axis=0)
        q_aug.append(jnp.concatenate([qa[u], qx], axis=1))
    bpt = tk // NSA_SEL_BLOCK
    for t in range(nb // bpt):
        used_ref[t] = (jnp.max(any_sel[t * bpt:(t + 1) * bpt, :]) > 0.5 * NEG_INF).astype(jnp.int32)

    causal_add = [_causal_add(hpg, tq, c, False) for c in range(nqc)]
    future_add = [_causal_add(hpg, tq, c, True) for c in range(nqc)]
    wlen = NSA_WINDOW + tq
    wxt = wx_ref[pl.ds(d0, wlen), :]
    for u in grp:
        su = s_ref.at[u]
        kw_aug = jnp.concatenate([kw_ref[pl.ds(d0, wlen), kcol(u)], wxt], axis=1)
        su[:, 0:wlen] = lax.dot_general(q_aug0[u], kw_aug, _NT, preferred_element_type=F32)
        for c in range(nqc):
            lo = slice(c * LANES, (c + 1) * LANES)
            su[:, lo] = su[:, lo] + future_add[c]
            hi = slice(NSA_WINDOW + c * LANES, NSA_WINDOW + (c + 1) * LANES)
            su[:, hi] = su[:, hi] + causal_add[c]
    for u in grp:
        _online_tile(*stats(u), vw_ref[pl.ds(d0, wlen), kcol(u)], rows=rows, width=wlen, first=True)
        ob_ref[u, 2] = _finish(l_ref.at[u], acc_ref.at[u])

    kxd = _positions_only(kx_ref[pl.ds(d0, tq), :])
    for u in grp:
        su = s_ref.at[u]
        kd_aug = jnp.concatenate([ks_ref[pl.ds(d0, tq), kcol(u)], kxd], axis=1)
        su[:, 0:tq] = lax.dot_general(q_aug0[u], kd_aug, _NT, preferred_element_type=F32)
        for c in range(nqc):
            cs = slice(c * LANES, (c + 1) * LANES)
            su[:, cs] = su[:, cs] + causal_add[c]
    for u in grp:
        _online_tile(*stats(u), vs_ref[pl.ds(d0, tq), kcol(u)], rows=rows, width=tq, first=True)

    def sel_body(kt, carry):
        @pl.when(used_ref[kt] != 0)
        def _():
            k0 = pl.multiple_of(kt * tk, tk)
            kxt = kx_ref[pl.ds(k0, tk), :]
            for u in grp:
                k_aug = jnp.concatenate([ks_ref[pl.ds(k0, tk), kcol(u)], kxt], axis=1)
                s_ref[u, :, 0:tk] = lax.dot_general(q_aug[u], k_aug, _NT, preferred_element_type=F32)
            for u in grp:
                _online_tile(*stats(u), vs_ref[pl.ds(k0, tk), kcol(u)], rows=rows, width=tk, first=False)
        return carry

    lax.fori_loop(0, (t0 + tk - 1) // tk, sel_body, 0)
    for u in grp:
        ob_ref[u, 1] = _finish(l_ref.at[u], acc_ref.at[u])

    gates = gt_ref[...].astype(F32)
    for u in grp:
        gts = pltpu.roll(gates, (LANES - (gp * gps + u) * hpg) % LANES, 1)
        for h in range(hpg):
            rs = slice(h * tq, (h + 1) * tq)
            cs = slice(h * HEAD_DIM, (h + 1) * HEAD_DIM)
            mix = jnp.zeros((tq, HEAD_DIM), F32)
            for br in range(3):
                z = z_refs[br][u][:, cs].astype(F32)
                gate = gts[:, br * N_HEADS + h:br * N_HEADS + h + 1]
                mix = mix + gate * ob_ref[u, br, rs, :] * (z * _sigmoid(z))
            o_ref[:, u * gw + h * HEAD_DIM:u * gw + (h + 1) * HEAD_DIM] = mix.astype(o_ref.dtype)


def _nsa_attention(pieces, proj, gate_logits, win, kvc, ovl_t, *, batch, seq):
    m = proj.shape[0]
    tq = min(128, seq)
    tk = min(512, seq)
    nt = seq // tq
    nb = seq // NSA_SEL_BLOCK
    nc = seq // NSA_CMP_STRIDE
    hpg, g_n, dh = NSA_HPG, NSA_KV_HEADS, HEAD_DIM
    gps = NSA_GROUPS_PER_STEP
    ngp = g_n // gps
    gw = hpg * dh
    q_cols = N_HEADS * dh
    kv_col0 = q_cols // (gps * dh)
    z_col0 = (q_cols + 3 * 2 * g_n * dh) // gw
    rows = hpg * tq
    wlen = NSA_WINDOW + tq
    sq = pl.Squeezed()
    assert nb <= POS_COL and g_n % gps == 0
    kx = _key_extras(jnp.arange(seq), NSA_SEL_BLOCK)
    wx = _key_extras(jnp.arange(seq + NSA_WINDOW))
    cx = _key_extras(jnp.arange(nc) * NSA_CMP_STRIDE + (NSA_CMP_LEN - 1))

    def kv_spec(branch, kv):
        c = kv_col0 + (branch * 2 + kv) * ngp
        return pl.BlockSpec((seq, gps * dh), lambda b, g, i: (b, c + g))

    def z_spec(branch, u):
        c = z_col0 + branch * g_n + u
        return pl.BlockSpec((tq, gw), lambda b, g, i: (b * nt + i, c + g * gps))

    win_spec = lambda kv: pl.BlockSpec((sq, seq + NSA_WINDOW, gps * dh), lambda b, g, i: (b, 0, kv * ngp + g))
    cmp_spec = lambda kv: pl.BlockSpec((sq, gps, nc, dh), lambda b, g, i: (kv, b * ngp + g, 0, 0))
    full = lambda arr: pl.BlockSpec(arr.shape, lambda b, g, i: (0,) * arr.ndim)
    in_specs = (
        [pl.BlockSpec(memory_space=pltpu.SMEM),
         pl.BlockSpec((tq, gps * gw), lambda b, g, i: (b * nt + i, g))]
        + [z_spec(br, u) for br in range(3) for u in range(gps)]
        + [pl.BlockSpec((tq, LANES), lambda b, g, i: (b * nt + i, 0)),
           kv_spec(1, 0), kv_spec(1, 1),
           win_spec(0), win_spec(1),
           cmp_spec(0), cmp_spec(1),
           full(kx), full(wx), full(cx), full(ovl_t)])
    kern = functools.partial(_nsa_kernel, tq=tq, tk=tk, n_top=min(NSA_SEL_TOPN, nb), gps=gps)
    return pl.pallas_call(
        kern,
        grid=(batch, ngp, nt),
        in_specs=in_specs,
        out_specs=pl.BlockSpec((tq, gps * gw), lambda b, g, i: (b * nt + i, g)),
        out_shape=jax.ShapeDtypeStruct((m, q_cols), BF16),
        scratch_shapes=[pltpu.VMEM((gps, rows, max(wlen, tk)), F32),
                        pltpu.VMEM((gps, rows, max(wlen, tk)), BF16),
                        pltpu.VMEM((gps, rows, LANES), F32),
                        pltpu.VMEM((gps, rows, LANES), F32),
                        pltpu.VMEM((gps, rows, LANES), F32),
                        pltpu.VMEM((gps, rows, dh), F32),
                        pltpu.VMEM((gps, 3, rows, dh), F32),
                        pltpu.SMEM((seq // tk,), jnp.int32)],
        compiler_params=pltpu.CompilerParams(
            dimension_semantics=("arbitrary", "arbitrary", "arbitrary"), vmem_limit_bytes=VMEM_LIMIT),
        name="nsa_attention",
    )(pieces, proj, *([proj] * (3 * gps)), gate_logits, proj, proj, win, win, kvc, kvc, kx, wx, cx, ovl_t)


def _moba_kernel(sl_ref, q_ref, z_ref, k_ref, v_ref, mx_ref, o_ref,
                 km_ref, s_ref, p_ref, m_ref, l_ref, a_ref, acc_ref, *, tq, tk, n_top, gps):
    gp = pl.program_id(1)
    i = pl.program_id(2)
    hpg = MOBA_HPG
    gw = hpg * HEAD_DIM
    rows = hpg * tq
    seq = k_ref.shape[0]
    nbm = seq // MOBA_BLOCK
    bpt = tk // MOBA_BLOCK
    grp = range(gps)
    kcol = lambda u: slice(u * HEAD_DIM, (u + 1) * HEAD_DIM)
    stats = lambda u: (s_ref.at[u], p_ref.at[u], m_ref.at[u], l_ref.at[u], a_ref.at[u], acc_ref.at[u])

    @pl.when(i == 0)
    def _():
        blk = lax.broadcasted_iota(jnp.int32, (nbm, seq), 1) >> (MOBA_BLOCK.bit_length() - 1)
        avg = jnp.where(blk == lax.broadcasted_iota(jnp.int32, (nbm, seq), 0), 1.0 / MOBA_BLOCK, 0.0)
        for u in grp:
            km_ref[u] = jnp.dot(avg.astype(BF16), k_ref[:, kcol(u)], preferred_element_type=F32)

    n_io = lax.broadcasted_iota(jnp.int32, (nbm, rows), 0)
    past = n_io < i
    q_aug0, q_aug = [], []
    for u in grp:
        qa = _stack_heads(q_ref, hpg, u * gw)
        qx0 = _slope_extras(sl_ref, (gp * gps + u) * hpg, hpg, tq).astype(BF16)
        km = km_ref[u]
        km_hi = km.astype(BF16)
        km_lo = (km - km_hi.astype(F32)).astype(BF16)
        sb = (lax.dot_general(km_hi, qa, _NT, preferred_element_type=F32)
              + lax.dot_general(km_lo, qa, _NT, preferred_element_type=F32))
        sb = jnp.where(past, sb, NEG_INF)
        rank = jnp.zeros((nbm, rows), jnp.int32)
        for mm in range(nbm):
            row = sb[mm:mm + 1, :]
            beats = (row > sb) | ((row == sb) & (n_io > mm))
            rank = rank + beats.astype(jnp.int32)
        selneg = jnp.where(past & (rank < n_top), 0.0, NEG_INF).astype(F32)
        qx = qx0 + _pad_rows_to_lanes(selneg).T.astype(BF16)
        q_aug0.append(jnp.concatenate([qa, qx0], axis=1))
        q_aug.append(jnp.concatenate([qa, qx], axis=1))

    d0 = pl.multiple_of(i * tq, tq)
    mxd = _positions_only(mx_ref[pl.ds(d0, tq), :])
    causal_add = [_causal_add(hpg, tq, c, False) for c in range(tq // LANES)]
    for u in grp:
        su = s_ref.at[u]
        kd_aug = jnp.concatenate([k_ref[pl.ds(d0, tq), kcol(u)], mxd], axis=1)
        su[:, 0:tq] = lax.dot_general(q_aug0[u], kd_aug, _NT, preferred_element_type=F32)
        for c in range(tq // LANES):
            cs = slice(c * LANES, (c + 1) * LANES)
            su[:, cs] = su[:, cs] + causal_add[c]
    for u in grp:
        _online_tile(*stats(u), v_ref[pl.ds(d0, tq), kcol(u)], rows=rows, width=tq, first=True)

    def body(t, carry):
        k0 = pl.multiple_of(t * tk, tk)
        mxt = mx_ref[pl.ds(k0, tk), :]
        for u in grp:
            k_aug = jnp.concatenate([k_ref[pl.ds(k0, tk), kcol(u)], mxt], axis=1)
            s_ref[u] = lax.dot_general(q_aug[u], k_aug, _NT, preferred_element_type=F32)
        for u in grp:
            _online_tile(*stats(u), v_ref[pl.ds(k0, tk), kcol(u)], rows=rows, width=tk, first=False)
        return carry

    lax.fori_loop(0, (i + bpt - 1) // bpt, body, 0)
    for u in grp:
        o = _finish(l_ref.at[u], acc_ref.at[u])
        for h in range(hpg):
            cs = slice(u * gw + h * HEAD_DIM, u * gw + (h + 1) * HEAD_DIM)
            z = z_ref[:, cs].astype(F32)
            o_ref[:, cs] = (o[h * tq:(h + 1) * tq] * (z * _sigmoid(z))).astype(o_ref.dtype)


def _moba_attention(pieces, qz, kv, *, batch, seq):
    m = qz.shape[0]
    tq = MOBA_BLOCK
    tk = min(2 * MOBA_BLOCK, seq)
    nt = seq // tq
    hpg, g_n, dh = MOBA_HPG, MOBA_KV_HEADS, HEAD_DIM
    gps = MOBA_GROUPS_PER_STEP
    ngp = g_n // gps
    gw = gps * hpg * dh
    rows = hpg * tq
    nbm = seq // MOBA_BLOCK
    assert nbm <= POS_COL and g_n % gps == 0
    mx = _key_extras(jnp.arange(seq), MOBA_BLOCK)
    in_specs = [
        pl.BlockSpec(memory_space=pltpu.SMEM),
        pl.BlockSpec((tq, gw), lambda b, g, i: (b * nt + i, g)),
        pl.BlockSpec((tq, gw), lambda b, g, i: (b * nt + i, ngp + g)),
        pl.BlockSpec((seq, gps * dh), lambda b, g, i: (b, g)),
        pl.BlockSpec((seq, gps * dh), lambda b, g, i: (b, ngp + g)),
        pl.BlockSpec(mx.shape, lambda b, g, i: (0, 0)),
    ]
    kern = functools.partial(_moba_kernel, tq=tq, tk=tk, n_top=min(MOBA_TOPK, nbm), gps=gps)
    return pl.pallas_call(
        kern,
        grid=(batch, ngp, nt),
        in_specs=in_specs,
        out_specs=pl.BlockSpec((tq, gw), lambda b, g, i: (b * nt + i, g)),
        out_shape=jax.ShapeDtypeStruct((m, N_HEADS * dh), BF16),
        scratch_shapes=[pltpu.VMEM((gps, nbm, dh), F32),
                        pltpu.VMEM((gps, rows, tk), F32),
                        pltpu.VMEM((gps, rows, tk), BF16),
                        pltpu.VMEM((gps, rows, LANES), F32),
                        pltpu.VMEM((gps, rows, LANES), F32),
                        pltpu.VMEM((gps, rows, LANES), F32),
                        pltpu.VMEM((gps, rows, dh), F32)],
        compiler_params=pltpu.CompilerParams(
            dimension_semantics=("arbitrary", "arbitrary", "arbitrary"), vmem_limit_bytes=VMEM_LIMIT),
        name="moba_attention",
    )(pieces, qz, qz, kv, kv, mx)


def _nsa_layer(h, norm_g, w_in, pos_k, pos_v, w1_k, w2_k, w1_v, w2_v, w_out, *, batch, seq):
    dh, g_n = HEAD_DIM, NSA_KV_HEADS
    q_cols = N_HEADS * dh
    kv_cols = 3 * 2 * g_n * dh
    n_main = q_cols + kv_cols + 3 * q_cols
    n_gate = w_in.shape[1] - n_main
    assert n_gate == 3 * N_HEADS <= LANES and n_main % LANES == 0
    (xn,) = _rmsnorm(h, norm_g[None, :], BF16)
    scale = jnp.where(jnp.arange(n_main) < q_cols, dh ** -0.5 * LOG2E, 1.0).astype(F32)[None, :]
    w_t = w_in.T
    proj = _matmul(xn, w_t, out_dtype=BF16, scale=scale, n_cols=n_main, w_is_nk=True,
                   name="nsa_in_proj")
    w_gate_t = jnp.pad(w_t[n_main:], ((0, LANES - n_gate), (0, 0)))
    gates = _matmul(xn, w_gate_t, out_dtype=BF16, scale=jnp.ones((1, LANES), F32), sigmoid=True,
                    w_is_nk=True, name="nsa_gate_proj")

    half = NSA_CMP_STRIDE
    nr = seq // half
    pos = jnp.stack([pos_k, pos_v])
    pa = pos[:, :half].reshape(2, 1, half * dh)
    pb = pos[:, half:].reshape(2, 1, half * dh)
    w1 = jnp.stack([w1_k, w1_v])
    w1a = w1[:, :half].reshape(2, half * dh, dh).astype(BF16)
    w1b = w1[:, half:].reshape(2, half * dh, dh).astype(BF16)
    w2 = jnp.stack([w2_k, w2_v]).astype(BF16)
    kvc = _compress(proj, q_cols, pa, pb, w1a, w1b, w2, batch=batch, seq=seq)

    wcol = q_cols + 2 * 2 * g_n * dh
    win = proj[:, wcol:wcol + 2 * g_n * dh].reshape(batch, seq, 2 * g_n * dh)
    win = jnp.pad(win, ((0, 0), (NSA_WINDOW, 0), (0, 0)))

    nb = seq // NSA_SEL_BLOCK
    cstart = jnp.arange(nr)[None, :] * NSA_CMP_STRIDE
    sstart = jnp.arange(nb)[:, None] * NSA_SEL_BLOCK
    ovl_t = ((cstart <= sstart + NSA_SEL_BLOCK - 1)
             & (cstart + NSA_CMP_LEN - 1 >= sstart)
             & (jnp.arange(nr)[None, :] < (seq - NSA_CMP_LEN) // NSA_CMP_STRIDE + 1)).astype(BF16)
    mix = _nsa_attention(_slope_pieces(), proj, gates, win, kvc, ovl_t, batch=batch, seq=seq)
    return _matmul(mix, w_out, out_dtype=F32, residual=h, name="nsa_out_proj")


def _moba_shared_kv(xn_kv, kv_w):
    one = jnp.ones((1, kv_w.shape[1]), F32)
    return _matmul(xn_kv, kv_w, out_dtype=BF16, scale=one, name="moba_kv_proj")


def _moba_layer(h, xn_q, kv, w_in, w_out, *, batch, seq):
    dh = HEAD_DIM
    q_cols = N_HEADS * dh
    scale = jnp.where(jnp.arange(w_in.shape[1]) < q_cols, dh ** -0.5 * LOG2E, 1.0).astype(F32)[None, :]
    qz = _matmul(xn_q, w_in, out_dtype=BF16, scale=scale, name="moba_in_proj")
    o = _moba_attention(_slope_pieces(), qz, kv, batch=batch, seq=seq)
    return _matmul(o, w_out, out_dtype=F32, residual=h, name="moba_out_proj")


def kernel(x, a_norm_g, a_w_in, a_cmp_pos_k, a_cmp_pos_v, a_cmp_w1_k, a_cmp_w2_k, a_cmp_w1_v, a_cmp_w2_v,
           a_w_out, kv_norm_g, kv_w, b_norm_g, b_w_in, b_w_out, final_norm_g):
    batch, seq, d = x.shape
    h = x.reshape(batch * seq, d)
    for layer in range(a_norm_g.shape[0]):
        h = _nsa_layer(h, a_norm_g[layer], a_w_in[layer], a_cmp_pos_k[layer], a_cmp_pos_v[layer],
                       a_cmp_w1_k[layer], a_cmp_w2_k[layer], a_cmp_w1_v[layer], a_cmp_w2_v[layer],
                       a_w_out[layer], batch=batch, seq=seq)
    kv = None
    for layer in range(b_norm_g.shape[0]):
        if layer == 0:
            xn_kv, xn_q = _rmsnorm(h, jnp.stack([kv_norm_g, b_norm_g[layer]]), BF16)
            kv = _moba_shared_kv(xn_kv, kv_w)
        else:
            (xn_q,) = _rmsnorm(h, b_norm_g[layer][None, :], BF16)
        h = _moba_layer(h, xn_q, kv, b_w_in[layer], b_w_out[layer], batch=batch, seq=seq)
    (out,) = _rmsnorm(h, final_norm_g[None, :], F32)
    return out.reshape(batch, seq, d)
```

```python
import functools

import jax
import jax.numpy as jnp
from jax import lax
from jax.experimental import pallas as pl
from jax.experimental.pallas import tpu as pltpu

F32 = jnp.float32
BF16 = jnp.bfloat16

N_HEADS = 32
HEAD_DIM = 128
NSA_KV_HEADS = 4
NSA_HPG = N_HEADS // NSA_KV_HEADS
NSA_CMP_LEN = 32
NSA_CMP_STRIDE = 16
NSA_SEL_BLOCK = 64
NSA_SEL_TOPN = 16
NSA_WINDOW = 512
MOBA_KV_HEADS = 8
MOBA_HPG = N_HEADS // MOBA_KV_HEADS
MOBA_BLOCK = 256
MOBA_TOPK = 3
RMS_EPS = 1e-6
NEG_INF = -1e30
FORCE_SCORE = 1e9
LOG2E = 1.4426950408889634

LANES = 128
VMEM_LIMIT = 56 * 1024 * 1024
NSA_GROUPS_PER_STEP = 2
MOBA_GROUPS_PER_STEP = 4
POS_COL = 120
POS_SPLIT = 64

_NT = (((1,), (1,)), ((), ()))


def _sigmoid(x):
    return 1.0 / (1.0 + jnp.exp(-x))


def _rmsnorm_kernel(x_ref, g_ref, *o_refs):
    x = x_ref[...].astype(F32)
    y = x * lax.rsqrt(jnp.mean(x * x, axis=-1, keepdims=True) + RMS_EPS)
    for i, o_ref in enumerate(o_refs):
        o_ref[...] = (y * g_ref[i:i + 1, :]).astype(o_ref.dtype)


def _rmsnorm(x, gains, out_dtype):
    m, d = x.shape
    n = gains.shape[0]
    tm = min(256, m)
    outs = pl.pallas_call(
        _rmsnorm_kernel,
        grid=(m // tm,),
        in_specs=[pl.BlockSpec((tm, d), lambda i: (i, 0)),
                  pl.BlockSpec((n, d), lambda i: (0, 0))],
        out_specs=[pl.BlockSpec((tm, d), lambda i: (i, 0)) for _ in range(n)],
        out_shape=[jax.ShapeDtypeStruct((m, d), out_dtype) for _ in range(n)],
        compiler_params=pltpu.CompilerParams(
            dimension_semantics=("arbitrary",), vmem_limit_bytes=VMEM_LIMIT),
        name="rmsnorm",
    )(x, gains)
    return outs


def _mm_kernel(a_ref, w_ref, x_ref, o_ref, *, w_is_nk, epilogue):
    w = w_ref[...].astype(BF16)
    if w_is_nk:
        acc = lax.dot_general(a_ref[...], w, _NT, preferred_element_type=F32)
    else:
        acc = jnp.dot(a_ref[...], w, preferred_element_type=F32)
    if epilogue == "residual":
        out = x_ref[...] + acc
    elif epilogue == "sigmoid":
        out = _sigmoid(acc)
    else:
        out = acc * x_ref[...]
    o_ref[...] = out.astype(o_ref.dtype)


def _matmul(a, w, *, out_dtype, scale=None, residual=None, sigmoid=False, n_cols=None, w_is_nk=False, name):
    m, k = a.shape
    n = (w.shape[0] if w_is_nk else w.shape[1]) if n_cols is None else n_cols
    tn = min(512, n)
    assert n % tn == 0
    if n // tn >= 32:
        tm = min(2048, m)
        a_spec = pl.BlockSpec((tm, k), lambda i, j: (i, 0), pipeline_mode=pl.Buffered(1))
    else:
        tm = min(1024, m)
        a_spec = pl.BlockSpec((tm, k), lambda i, j: (i, 0))
    w_spec = pl.BlockSpec((tn, k), lambda i, j: (j, 0)) if w_is_nk else pl.BlockSpec((k, tn), lambda i, j: (0, j))
    o_spec = pl.BlockSpec((tm, tn), lambda i, j: (i, j))
    epilogue = "residual" if residual is not None else ("sigmoid" if sigmoid else "scale")
    kern = functools.partial(_mm_kernel, w_is_nk=w_is_nk, epilogue=epilogue)
    if residual is None:
        extra, extra_spec = scale, pl.BlockSpec((1, tn), lambda i, j: (0, j))
    else:
        extra, extra_spec = residual, o_spec
    return pl.pallas_call(
        kern,
        grid=(m // tm, n // tn),
        in_specs=[a_spec, w_spec, extra_spec],
        out_specs=o_spec,
        out_shape=jax.ShapeDtypeStruct((m, n), out_dtype),
        compiler_params=pltpu.CompilerParams(
            dimension_semantics=("arbitrary", "arbitrary"), vmem_limit_bytes=VMEM_LIMIT),
        name=name,
    )(a, w, extra)


def _cmp_kernel(raw_ref, pa_ref, pb_ref, w1a_ref, w1b_ref, w2_ref, o_ref, raw32_ref):
    half = NSA_CMP_STRIDE
    nr = raw_ref.shape[0] // half
    raw32_ref[...] = raw_ref[...].astype(F32)
    r = jnp.concatenate([raw32_ref[pl.ds(l, nr, stride=half), :] for l in range(half)], axis=1)
    xa = (r + pa_ref[...]).astype(BF16)
    xb = (r + pb_ref[...]).astype(BF16)
    ya = jnp.dot(xa, w1a_ref[...], preferred_element_type=F32)
    yb = jnp.dot(xb, w1b_ref[...], preferred_element_type=F32)
    hid = ya + pltpu.roll(yb, nr - 1, 0)
    hid = hid * _sigmoid(hid)
    o_ref[...] = jnp.dot(hid.astype(BF16), w2_ref[...], preferred_element_type=F32).astype(o_ref.dtype)


def _compress(proj, col0, pa, pb, w1a, w1b, w2, *, batch, seq):
    g_n, dh = NSA_KV_HEADS, HEAD_DIM
    nr = seq // NSA_CMP_STRIDE
    kk = NSA_CMP_STRIDE * dh
    cb = col0 // dh
    sq = pl.Squeezed()
    wspec = lambda shape: pl.BlockSpec((sq,) + shape, lambda t, i: (t, 0, 0))
    return pl.pallas_call(
        _cmp_kernel,
        grid=(2, batch * g_n),
        in_specs=[pl.BlockSpec((seq, dh), lambda t, i: (i // g_n, cb + t * g_n + i % g_n)),
                  wspec((1, kk)), wspec((1, kk)), wspec((kk, dh)), wspec((kk, dh)), wspec((dh, dh))],
        out_specs=pl.BlockSpec((sq, sq, nr, dh), lambda t, i: (t, i, 0, 0)),
        out_shape=jax.ShapeDtypeStruct((2, batch * g_n, nr, dh), BF16),
        scratch_shapes=[pltpu.VMEM((seq, dh), F32)],
        compiler_params=pltpu.CompilerParams(
            dimension_semantics=("arbitrary", "arbitrary"), vmem_limit_bytes=VMEM_LIMIT),
        name="nsa_compress",
    )(proj, pa, pb, w1a, w1b, w2)


def _stack_heads(ref, n_heads, col0=0):
    return jnp.concatenate(
        [ref[:, col0 + h * HEAD_DIM:col0 + (h + 1) * HEAD_DIM] for h in range(n_heads)], axis=0)


def _query_in_tile(n_heads, tq, width):
    assert tq & (tq - 1) == 0, "query tile must be a power of two"
    return lax.broadcasted_iota(jnp.int32, (n_heads * tq, width), 0) & (tq - 1)


def _slope_extras(sl_ref, head0, n_heads, tq):
    lane = lax.broadcasted_iota(jnp.int32, (tq, LANES), 1)
    blocks = []
    for h in range(n_heads):
        x = jnp.zeros((tq, LANES), F32)
        for piece in range(3):
            hit = (lane == POS_COL + piece) | (lane == POS_COL + 3 + piece)
            x = jnp.where(hit, sl_ref[piece, head0 + h], x)
        blocks.append(x)
    return jnp.concatenate(blocks, axis=0)


def _causal_add(n_heads, tq, chunk, strict_future):
    qq = _query_in_tile(n_heads, tq, LANES)
    kk = lax.broadcasted_iota(jnp.int32, (n_heads * tq, LANES), 1) + chunk * LANES
    keep = (kk > qq) if strict_future else (kk <= qq)
    return jnp.where(keep, 0.0, NEG_INF)


def _positions_only(kx_tile):
    lane = lax.broadcasted_iota(jnp.int32, kx_tile.shape, 1)
    return jnp.where(lane >= POS_COL, kx_tile, jnp.zeros_like(kx_tile))


def _pad_rows_to_lanes(x):
    n = x.shape[0]
    assert n <= POS_COL, "block one-hot columns must stay clear of the slope/position columns"
    return jnp.concatenate([x, jnp.zeros((LANES - n, x.shape[1]), x.dtype)], axis=0)


def _online_tile(s_ref, p_ref, m_ref, l_ref, a_ref, acc_ref, v, *, rows, width, first):
    nch = width // LANES
    mt = s_ref[:, 0:LANES]
    for c in range(1, nch):
        mt = jnp.maximum(mt, s_ref[:, c * LANES:(c + 1) * LANES])
    mrow = jnp.max(mt, axis=-1, keepdims=True)
    if first:
        m_new = jnp.broadcast_to(mrow, (rows, LANES))
    else:
        m_old = m_ref[...]
        m_new = jnp.maximum(m_old, mrow)
        a_ref[...] = jnp.exp2(m_old - m_new)
    m_ref[...] = m_new
    ps = None
    for c in range(nch):
        cs = slice(c * LANES, (c + 1) * LANES)
        p = jnp.exp2(s_ref[:, cs] - m_new)
        p_ref[:, cs] = p.astype(BF16)
        ps = p if ps is None else ps + p
    if first:
        l_ref[...] = ps
    else:
        l_ref[...] = a_ref[...] * l_ref[...] + ps
    pv = jnp.dot(p_ref[:, 0:width], v, preferred_element_type=F32)
    if first:
        acc_ref[...] = pv
    else:
        acc_ref[...] = a_ref[...] * acc_ref[...] + pv


def _finish(l_ref, acc_ref):
    return acc_ref[...] * (1.0 / jnp.sum(l_ref[...], axis=-1, keepdims=True))


def _key_extras(pos, block=None):
    pos = pos[:, None]
    col = jnp.arange(LANES)[None, :]
    x = jnp.zeros((pos.shape[0], LANES), F32)
    if block is not None:
        x = jnp.where(col == pos // block, 1.0, x)
    x = jnp.where((col >= POS_COL) & (col < POS_COL + 3), (pos // POS_SPLIT) * POS_SPLIT, x)
    x = jnp.where((col >= POS_COL + 3) & (col < POS_COL + 6), pos % POS_SPLIT, x)
    return x.astype(BF16)


def _slope_pieces():
    s = 2.0 ** (-8.0 * jnp.arange(1, N_HEADS + 1, dtype=F32) / N_HEADS) * LOG2E
    a = s.astype(BF16).astype(F32)
    b = (s - a).astype(BF16).astype(F32)
    c = (s - a - b).astype(BF16).astype(F32)
    return jnp.stack([a, b, c])


def _nsa_kernel(sl_ref, q_ref, *refs, tq, tk, n_top, gps):
    z_refs = [refs[br * gps:(br + 1) * gps] for br in range(3)]
    (gt_ref, ks_ref, vs_ref, kw_ref, vw_ref, kc_ref, vc_ref, kx_ref, wx_ref, cx_ref, ovl_ref, gx_ref, o_ref,
     s_ref, p_ref, m_ref, l_ref, a_ref, acc_ref, sw_ref, pw_ref, mw_ref, lw_ref, accw_ref,
     ob_ref, used_ref) = refs[3 * gps:]
    gp = pl.program_id(1)
    i = pl.program_id(2)
    t0 = i * tq
    hpg = NSA_HPG
    gw = hpg * HEAD_DIM
    rows = hpg * tq
    grp = range(gps)
    kcol = lambda u: slice(u * HEAD_DIM, (u + 1) * HEAD_DIM)
    stats = lambda u: (s_ref.at[u], p_ref.at[u], m_ref.at[u], l_ref.at[u], a_ref.at[u], acc_ref.at[u])
    nqc = tq // LANES
    d0 = pl.multiple_of(t0, tq)
    qa = [_stack_heads(q_ref, hpg, u * gw) for u in grp]
    qx0 = [_slope_extras(sl_ref, (gp * gps + u) * hpg, hpg, tq).astype(BF16) for u in grp]
    q_aug0 = [jnp.concatenate([qa[u], qx0[u]], axis=1) for u in grp]

    nc = kc_ref.shape[1]
    cx = cx_ref[...]
    cend = lax.broadcasted_iota(jnp.int32, (1, nc), 1) * NSA_CMP_STRIDE + (NSA_CMP_LEN - 1)
    tqv = t0 + (lax.broadcasted_iota(jnp.int32, (rows, 1), 0) & (tq - 1))
    mask = cend <= tqv
    ovl = ovl_ref[...]
    nb = ovl.shape[0]
    j = lax.broadcasted_iota(jnp.int32, (nb, tq), 0)
    blkq = (t0 + lax.broadcasted_iota(jnp.int32, (nb, tq), 1)) >> (NSA_SEL_BLOCK.bit_length() - 1)
    forced = (j == 0) | (j == blkq) | (j == blkq - 1)
    blk0 = t0 >> (NSA_SEL_BLOCK.bit_length() - 1)
    q_aug = []
    any_sel = None
    for u in grp:
        kc_aug = jnp.concatenate([kc_ref[u], cx], axis=1)
        s = lax.dot_general(q_aug0[u], kc_aug, _NT, preferred_element_type=F32)
        s = jnp.where(mask, s, NEG_INF)
        mx = jnp.max(s, axis=-1, keepdims=True)
        e = jnp.where(mask, jnp.exp2(s - mx), 0.0)
        den = jnp.sum(e, axis=-1, keepdims=True)
        p = e * (1.0 / jnp.where(den > 0, den, 1.0))
        ob_ref[u, 0] = jnp.dot(p.astype(BF16), vc_ref[u], preferred_element_type=F32)
        psum = p[0:tq]
        for h in range(1, hpg):
            psum = psum + p[h * tq:(h + 1) * tq]
        p_hi = psum.astype(BF16)
        p_lo = (psum - p_hi.astype(F32)).astype(BF16)
        imp = (lax.dot_general(ovl, p_hi, _NT, preferred_element_type=F32)
               + lax.dot_general(ovl, p_lo, _NT, preferred_element_type=F32))
        imp = jnp.where(forced, FORCE_SCORE, jnp.where(j > blkq, NEG_INF, imp))
        rank = jnp.zeros((nb, tq), jnp.int32)
        for ii in range(nb):
            row = imp[ii:ii + 1, :]
            beats = (row > imp) | ((row == imp) & (j > ii))
            rank = rank + beats.astype(jnp.int32)
        selneg = jnp.where((rank < n_top) & (j < blk0), 0.0, NEG_INF).astype(F32)
        any_sel = selneg if any_sel is None else jnp.maximum(any_sel, selneg)
        selneg_q = _pad_rows_to_lanes(selneg).T.astype(BF16)
        qx = qx0[u] + jnp.concatenate([selneg_q] * hpg, axis=0)
        q_aug.append(jnp.concatenate([qa[u], qx], axis=1))
    bpt = tk // NSA_SEL_BLOCK
    for t in range(nb // bpt):
        used_ref[t] = (jnp.max(any_sel[t * bpt:(t + 1) * bpt, :]) > 0.5 * NEG_INF).astype(jnp.int32)

    causal_add = [_causal_add(hpg, tq, c, False) for c in range(nqc)]
    future_add = [_causal_add(hpg, tq, c, True) for c in range(nqc)]
    wlen = NSA_WINDOW + tq
    wxt = wx_ref[pl.ds(d0, wlen), :]
    for u in grp:
        su = sw_ref.at[u]
        kw_aug = jnp.concatenate([kw_ref[pl.ds(d0, wlen), kcol(u)], wxt], axis=1)
        su[...] = lax.dot_general(q_aug0[u], kw_aug, _NT, preferred_element_type=F32)
        for c in range(nqc):
            lo = slice(c * LANES, (c + 1) * LANES)
            su[:, lo] = su[:, lo] + future_add[c]
            hi = slice(NSA_WINDOW + c * LANES, NSA_WINDOW + (c + 1) * LANES)
            su[:, hi] = su[:, hi] + causal_add[c]
    for u in grp:
        _online_tile(sw_ref.at[u], pw_ref.at[u], mw_ref.at[u], lw_ref.at[u], None, accw_ref.at[u],
                     vw_ref[pl.ds(d0, wlen), kcol(u)], rows=rows, width=wlen, first=True)
        ob_ref[u, 2] = _finish(lw_ref.at[u], accw_ref.at[u])

    kxd = _positions_only(kx_ref[pl.ds(d0, tq), :])
    for u in grp:
        su = s_ref.at[u]
        kd_aug = jnp.concatenate([ks_ref[pl.ds(d0, tq), kcol(u)], kxd], axis=1)
        su[:, 0:tq] = lax.dot_general(q_aug0[u], kd_aug, _NT, preferred_element_type=F32)
        for c in range(nqc):
            cs = slice(c * LANES, (c + 1) * LANES)
            su[:, cs] = su[:, cs] + causal_add[c]
    for u in grp:
        _online_tile(*stats(u), vs_ref[pl.ds(d0, tq), kcol(u)], rows=rows, width=tq, first=True)

    def sel_body(kt, carry):
        @pl.when(used_ref[kt] != 0)
        def _():
            k0 = pl.multiple_of(kt * tk, tk)
            kxt = kx_ref[pl.ds(k0, tk), :]
            for u in grp:
                k_aug = jnp.concatenate([ks_ref[pl.ds(k0, tk), kcol(u)], kxt], axis=1)
                s_ref[u, :, 0:tk] = lax.dot_general(q_aug[u], k_aug, _NT, preferred_element_type=F32)
            for u in grp:
                _online_tile(*stats(u), vs_ref[pl.ds(k0, tk), kcol(u)], rows=rows, width=tk, first=False)
        return carry

    lax.fori_loop(0, (t0 + tk - 1) // tk, sel_body, 0)
    for u in grp:
        ob_ref[u, 1] = _finish(l_ref.at[u], acc_ref.at[u])

    gates = gt_ref[...].astype(F32)
    for u in grp:
        gts = pltpu.roll(gates, (LANES - (gp * gps + u) * hpg) % LANES, 1)
        gfull = jnp.dot(gts.astype(BF16), gx_ref[...], preferred_element_type=F32)
        for h in range(hpg):
            rs = slice(h * tq, (h + 1) * tq)
            cs = slice(h * HEAD_DIM, (h + 1) * HEAD_DIM)
            mix = jnp.zeros((tq, HEAD_DIM), F32)
            for br in range(3):
                z = z_refs[br][u][:, cs].astype(F32)
                gate = gfull[:, (br * hpg + h) * LANES:(br * hpg + h + 1) * LANES]
                mix = mix + gate * ob_ref[u, br, rs, :] * (z * _sigmoid(z))
            o_ref[:, u * gw + h * HEAD_DIM:u * gw + (h + 1) * HEAD_DIM] = mix.astype(o_ref.dtype)


def _nsa_attention(pieces, proj, gate_logits, win, kvc, ovl_t, *, batch, seq):
    m = proj.shape[0]
    tq = min(128, seq)
    tk = min(512, seq)
    nt = seq // tq
    nb = seq // NSA_SEL_BLOCK
    nc = seq // NSA_CMP_STRIDE
    hpg, g_n, dh = NSA_HPG, NSA_KV_HEADS, HEAD_DIM
    gps = NSA_GROUPS_PER_STEP
    ngp = g_n // gps
    gw = hpg * dh
    q_cols = N_HEADS * dh
    kv_col0 = q_cols // (gps * dh)
    z_col0 = (q_cols + 3 * 2 * g_n * dh) // gw
    rows = hpg * tq
    wlen = NSA_WINDOW + tq
    sq = pl.Squeezed()
    assert nb <= POS_COL and g_n % gps == 0
    kx = _key_extras(jnp.arange(seq), NSA_SEL_BLOCK)
    wx = _key_extras(jnp.arange(seq + NSA_WINDOW))
    cx = _key_extras(jnp.arange(nc) * NSA_CMP_STRIDE + (NSA_CMP_LEN - 1))
    gsrc = (jnp.arange(3 * hpg) // hpg) * N_HEADS + jnp.arange(3 * hpg) % hpg
    gx = (jnp.arange(LANES)[:, None] == jnp.repeat(gsrc, LANES)[None, :]).astype(BF16)

    once = pl.Buffered(1)

    def kv_spec(branch, kv):
        c = kv_col0 + (branch * 2 + kv) * ngp
        return pl.BlockSpec((seq, gps * dh), lambda b, g, i: (b, c + g), pipeline_mode=once)

    def z_spec(branch, u):
        c = z_col0 + branch * g_n + u
        return pl.BlockSpec((tq, gw), lambda b, g, i: (b * nt + i, c + g * gps))

    win_spec = lambda kv: pl.BlockSpec((sq, seq + NSA_WINDOW, gps * dh), lambda b, g, i: (b, 0, kv * ngp + g),
                                       pipeline_mode=once)
    cmp_spec = lambda kv: pl.BlockSpec((sq, gps, nc, dh), lambda b, g, i: (kv, b * ngp + g, 0, 0))
    full = lambda arr: pl.BlockSpec(arr.shape, lambda b, g, i: (0,) * arr.ndim, pipeline_mode=once)
    in_specs = (
        [pl.BlockSpec(memory_space=pltpu.SMEM),
         pl.BlockSpec((tq, gps * gw), lambda b, g, i: (b * nt + i, g))]
        + [z_spec(br, u) for br in range(3) for u in range(gps)]
        + [pl.BlockSpec((tq, LANES), lambda b, g, i: (b * nt + i, 0)),
           kv_spec(1, 0), kv_spec(1, 1),
           win_spec(0), win_spec(1),
           cmp_spec(0), cmp_spec(1),
           full(kx), full(wx), full(cx), full(ovl_t), full(gx)])
    kern = functools.partial(_nsa_kernel, tq=tq, tk=tk, n_top=min(NSA_SEL_TOPN, nb), gps=gps)
    return pl.pallas_call(
        kern,
        grid=(batch, ngp, nt),
        in_specs=in_specs,
        out_specs=pl.BlockSpec((tq, gps * gw), lambda b, g, i: (b * nt + i, g)),
        out_shape=jax.ShapeDtypeStruct((m, q_cols), BF16),
        scratch_shapes=[pltpu.VMEM((gps, rows, tk), F32),
                        pltpu.VMEM((gps, rows, tk), BF16),
                        pltpu.VMEM((gps, rows, LANES), F32),
                        pltpu.VMEM((gps, rows, LANES), F32),
                        pltpu.VMEM((gps, rows, LANES), F32),
                        pltpu.VMEM((gps, rows, dh), F32),
                        pltpu.VMEM((gps, rows, wlen), F32),
                        pltpu.VMEM((gps, rows, wlen), BF16),
                        pltpu.VMEM((gps, rows, LANES), F32),
                        pltpu.VMEM((gps, rows, LANES), F32),
                        pltpu.VMEM((gps, rows, dh), F32),
                        pltpu.VMEM((gps, 3, rows, dh), F32),
                        pltpu.SMEM((seq // tk,), jnp.int32)],
        compiler_params=pltpu.CompilerParams(
            dimension_semantics=("arbitrary", "arbitrary", "arbitrary"), vmem_limit_bytes=VMEM_LIMIT),
        name="nsa_attention",
    )(pieces, proj, *([proj] * (3 * gps)), gate_logits, proj, proj, win, win, kvc, kvc, kx, wx, cx, ovl_t, gx)


def _moba_kernel(sl_ref, q_ref, z_ref, k_ref, v_ref, mx_ref, o_ref,
                 km_ref, s_ref, p_ref, m_ref, l_ref, a_ref, acc_ref, *, tq, tk, n_top, gps):
    gp = pl.program_id(1)
    i = pl.program_id(2)
    hpg = MOBA_HPG
    gw = hpg * HEAD_DIM
    rows = hpg * tq
    seq = k_ref.shape[0]
    nbm = seq // MOBA_BLOCK
    bpt = tk // MOBA_BLOCK
    grp = range(gps)
    kcol = lambda u: slice(u * HEAD_DIM, (u + 1) * HEAD_DIM)
    stats = lambda u: (s_ref.at[u], p_ref.at[u], m_ref.at[u], l_ref.at[u], a_ref.at[u], acc_ref.at[u])

    @pl.when(i == 0)
    def _():
        blk = lax.broadcasted_iota(jnp.int32, (nbm, seq), 1) >> (MOBA_BLOCK.bit_length() - 1)
        avg = jnp.where(blk == lax.broadcasted_iota(jnp.int32, (nbm, seq), 0), 1.0 / MOBA_BLOCK, 0.0)
        for u in grp:
            km_ref[u] = jnp.dot(avg.astype(BF16), k_ref[:, kcol(u)], preferred_element_type=F32)

    n_io = lax.broadcasted_iota(jnp.int32, (nbm, rows), 0)
    past = n_io < i
    q_aug0, q_aug = [], []
    for u in grp:
        qa = _stack_heads(q_ref, hpg, u * gw)
        qx0 = _slope_extras(sl_ref, (gp * gps + u) * hpg, hpg, tq).astype(BF16)
        km = km_ref[u]
        km_hi = km.astype(BF16)
        km_lo = (km - km_hi.astype(F32)).astype(BF16)
        sb = (lax.dot_general(km_hi, qa, _NT, preferred_element_type=F32)
              + lax.dot_general(km_lo, qa, _NT, preferred_element_type=F32))
        sb = jnp.where(past, sb, NEG_INF)
        rank = jnp.zeros((nbm, rows), jnp.int32)
        for mm in range(nbm):
            row = sb[mm:mm + 1, :]
            beats = (row > sb) | ((row == sb) & (n_io > mm))
            rank = rank + beats.astype(jnp.int32)
        selneg = jnp.where(past & (rank < n_top), 0.0, NEG_INF).astype(F32)
        qx = qx0 + _pad_rows_to_lanes(selneg).T.astype(BF16)
        q_aug0.append(jnp.concatenate([qa, qx0], axis=1))
        q_aug.append(jnp.concatenate([qa, qx], axis=1))

    d0 = pl.multiple_of(i * tq, tq)
    mxd = _positions_only(mx_ref[pl.ds(d0, tq), :])
    causal_add = [_causal_add(hpg, tq, c, False) for c in range(tq // LANES)]
    for u in grp:
        su = s_ref.at[u]
        kd_aug = jnp.concatenate([k_ref[pl.ds(d0, tq), kcol(u)], mxd], axis=1)
        su[:, 0:tq] = lax.dot_general(q_aug0[u], kd_aug, _NT, preferred_element_type=F32)
        for c in range(tq // LANES):
            cs = slice(c * LANES, (c + 1) * LANES)
            su[:, cs] = su[:, cs] + causal_add[c]
    for u in grp:
        _online_tile(*stats(u), v_ref[pl.ds(d0, tq), kcol(u)], rows=rows, width=tq, first=True)

    def body(t, carry):
        k0 = pl.multiple_of(t * tk, tk)
        mxt = mx_ref[pl.ds(k0, tk), :]
        for u in grp:
            k_aug = jnp.concatenate([k_ref[pl.ds(k0, tk), kcol(u)], mxt], axis=1)
            s_ref[u] = lax.dot_general(q_aug[u], k_aug, _NT, preferred_element_type=F32)
        for u in grp:
            _online_tile(*stats(u), v_ref[pl.ds(k0, tk), kcol(u)], rows=rows, width=tk, first=False)
        return carry

    lax.fori_loop(0, (i + bpt - 1) // bpt, body, 0)
    for u in grp:
        o = _finish(l_ref.at[u], acc_ref.at[u])
        for h in range(hpg):
            cs = slice(u * gw + h * HEAD_DIM, u * gw + (h + 1) * HEAD_DIM)
            z = z_ref[:, cs].astype(F32)
            o_ref[:, cs] = (o[h * tq:(h + 1) * tq] * (z * _sigmoid(z))).astype(o_ref.dtype)


def _moba_attention(pieces, qz, kv, *, batch, seq):
    m = qz.shape[0]
    tq = MOBA_BLOCK
    tk = min(2 * MOBA_BLOCK, seq)
    nt = seq // tq
    hpg, g_n, dh = MOBA_HPG, MOBA_KV_HEADS, HEAD_DIM
    gps = MOBA_GROUPS_PER_STEP
    ngp = g_n // gps
    gw = gps * hpg * dh
    rows = hpg * tq
    nbm = seq // MOBA_BLOCK
    assert nbm <= POS_COL and g_n % gps == 0
    mx = _key_extras(jnp.arange(seq), MOBA_BLOCK)
    in_specs = [
        pl.BlockSpec(memory_space=pltpu.SMEM),
        pl.BlockSpec((tq, gw), lambda b, g, i: (b * nt + i, g)),
        pl.BlockSpec((tq, gw), lambda b, g, i: (b * nt + i, ngp + g)),
        pl.BlockSpec((seq, gps * dh), lambda b, g, i: (b, g)),
        pl.BlockSpec((seq, gps * dh), lambda b, g, i: (b, ngp + g)),
        pl.BlockSpec(mx.shape, lambda b, g, i: (0, 0)),
    ]
    kern = functools.partial(_moba_kernel, tq=tq, tk=tk, n_top=min(MOBA_TOPK, nbm), gps=gps)
    return pl.pallas_call(
        kern,
        grid=(batch, ngp, nt),
        in_specs=in_specs,
        out_specs=pl.BlockSpec((tq, gw), lambda b, g, i: (b * nt + i, g)),
        out_shape=jax.ShapeDtypeStruct((m, N_HEADS * dh), BF16),
        scratch_shapes=[pltpu.VMEM((gps, nbm, dh), F32),
                        pltpu.VMEM((gps, rows, tk), F32),
                        pltpu.VMEM((gps, rows, tk), BF16),
                        pltpu.VMEM((gps, rows, LANES), F32),
                        pltpu.VMEM((gps, rows, LANES), F32),
                        pltpu.VMEM((gps, rows, LANES), F32),
                        pltpu.VMEM((gps, rows, dh), F32)],
        compiler_params=pltpu.CompilerParams(
            dimension_semantics=("arbitrary", "arbitrary", "arbitrary"), vmem_limit_bytes=VMEM_LIMIT),
        name="moba_attention",
    )(pieces, qz, qz, kv, kv, mx)


def _nsa_layer(h, norm_g, w_in, pos_k, pos_v, w1_k, w2_k, w1_v, w2_v, w_out, *, batch, seq):
    dh, g_n = HEAD_DIM, NSA_KV_HEADS
    q_cols = N_HEADS * dh
    kv_cols = 3 * 2 * g_n * dh
    n_main = q_cols + kv_cols + 3 * q_cols
    n_gate = w_in.shape[1] - n_main
    assert n_gate == 3 * N_HEADS <= LANES and n_main % LANES == 0
    (xn,) = _rmsnorm(h, norm_g[None, :], BF16)
    scale = jnp.where(jnp.arange(n_main) < q_cols, dh ** -0.5 * LOG2E, 1.0).astype(F32)[None, :]
    w_t = w_in.T
    proj = _matmul(xn, w_t, out_dtype=BF16, scale=scale, n_cols=n_main, w_is_nk=True,
                   name="nsa_in_proj")
    w_gate_t = jnp.pad(w_t[n_main:], ((0, LANES - n_gate), (0, 0)))
    gates = _matmul(xn, w_gate_t, out_dtype=BF16, scale=jnp.ones((1, LANES), F32), sigmoid=True,
                    w_is_nk=True, name="nsa_gate_proj")

    half = NSA_CMP_STRIDE
    nr = seq // half
    pos = jnp.stack([pos_k, pos_v])
    pa = pos[:, :half].reshape(2, 1, half * dh)
    pb = pos[:, half:].reshape(2, 1, half * dh)
    w1 = jnp.stack([w1_k, w1_v])
    w1a = w1[:, :half].reshape(2, half * dh, dh).astype(BF16)
    w1b = w1[:, half:].reshape(2, half * dh, dh).astype(BF16)
    w2 = jnp.stack([w2_k, w2_v]).astype(BF16)
    kvc = _compress(proj, q_cols, pa, pb, w1a, w1b, w2, batch=batch, seq=seq)

    wcol = q_cols + 2 * 2 * g_n * dh
    win = proj[:, wcol:wcol + 2 * g_n * dh].reshape(batch, seq, 2 * g_n * dh)
    win = jnp.pad(win, ((0, 0), (NSA_WINDOW, 0), (0, 0)))

    nb = seq // NSA_SEL_BLOCK
    cstart = jnp.arange(nr)[None, :] * NSA_CMP_STRIDE
    sstart = jnp.arange(nb)[:, None] * NSA_SEL_BLOCK
    ovl_t = ((cstart <= sstart + NSA_SEL_BLOCK - 1)
             & (cstart + NSA_CMP_LEN - 1 >= sstart)
             & (jnp.arange(nr)[None, :] < (seq - NSA_CMP_LEN) // NSA_CMP_STRIDE + 1)).astype(BF16)
    mix = _nsa_attention(_slope_pieces(), proj, gates, win, kvc, ovl_t, batch=batch, seq=seq)
    return _matmul(mix, w_out, out_dtype=F32, residual=h, name="nsa_out_proj")


def _moba_shared_kv(xn_kv, kv_w):
    one = jnp.ones((1, kv_w.shape[1]), F32)
    return _matmul(xn_kv, kv_w, out_dtype=BF16, scale=one, name="moba_kv_proj")


def _moba_layer(h, xn_q, kv, w_in, w_out, *, batch, seq):
    dh = HEAD_DIM
    q_cols = N_HEADS * dh
    scale = jnp.where(jnp.arange(w_in.shape[1]) < q_cols, dh ** -0.5 * LOG2E, 1.0).astype(F32)[None, :]
    qz = _matmul(xn_q, w_in, out_dtype=BF16, scale=scale, name="moba_in_proj")
    o = _moba_attention(_slope_pieces(), qz, kv, batch=batch, seq=seq)
    return _matmul(o, w_out, out_dtype=F32, residual=h, name="moba_out_proj")


def kernel(x, a_norm_g, a_w_in, a_cmp_pos_k, a_cmp_pos_v, a_cmp_w1_k, a_cmp_w2_k, a_cmp_w1_v, a_cmp_w2_v,
           a_w_out, kv_norm_g, kv_w, b_norm_g, b_w_in, b_w_out, final_norm_g):
    batch, seq, d = x.shape
    h = x.reshape(batch * seq, d)
    for layer in range(a_norm_g.shape[0]):
        h = _nsa_layer(h, a_norm_g[layer], a_w_in[layer], a_cmp_pos_k[layer], a_cmp_pos_v[layer],
                       a_cmp_w1_k[layer], a_cmp_w2_k[layer], a_cmp_w1_v[layer], a_cmp_w2_v[layer],
                       a_w_out[layer], batch=batch, seq=seq)
    kv = None
    for layer in range(b_norm_g.shape[0]):
        if layer == 0:
            xn_kv, xn_q = _rmsnorm(h, jnp.stack([kv_norm_g, b_norm_g[layer]]), BF16)
            kv = _moba_shared_kv(xn_kv, kv_w)
        else:
            (xn_q,) = _rmsnorm(h, b_norm_g[layer][None, :], BF16)
        h = _moba_layer(h, xn_q, kv, b_w_in[layer], b_w_out[layer], batch=batch, seq=seq)
    (out,) = _rmsnorm(h, final_norm_g[None, :], F32)
    return out.reshape(batch, seq, d)
```

```python
import functools

import jax
import jax.numpy as jnp
from jax import lax
from jax.experimental import pallas as pl
from jax.experimental.pallas import tpu as pltpu

F32 = jnp.float32
BF16 = jnp.bfloat16

N_HEADS = 32
HEAD_DIM = 128
NSA_KV_HEADS = 4
NSA_HPG = N_HEADS // NSA_KV_HEADS
NSA_CMP_LEN = 32
NSA_CMP_STRIDE = 16
NSA_SEL_BLOCK = 64
NSA_SEL_TOPN = 16
NSA_WINDOW = 512
MOBA_KV_HEADS = 8
MOBA_HPG = N_HEADS // MOBA_KV_HEADS
MOBA_BLOCK = 256
MOBA_TOPK = 3
RMS_EPS = 1e-6
NEG_INF = -1e30
FORCE_SCORE = 1e9
LOG2E = 1.4426950408889634

LANES = 128
VMEM_LIMIT = 56 * 1024 * 1024
NSA_GROUPS_PER_STEP = 2
MOBA_GROUPS_PER_STEP = 4
POS_COL = 120
POS_SPLIT = 64

_NT = (((1,), (1,)), ((), ()))


def _sigmoid(x):
    return 1.0 / (1.0 + jnp.exp(-x))


def _rmsnorm_kernel(x_ref, g_ref, *o_refs):
    x = x_ref[...].astype(F32)
    y = x * lax.rsqrt(jnp.mean(x * x, axis=-1, keepdims=True) + RMS_EPS)
    for i, o_ref in enumerate(o_refs):
        o_ref[...] = (y * g_ref[i:i + 1, :]).astype(o_ref.dtype)


def _rmsnorm(x, gains, out_dtype):
    m, d = x.shape
    n = gains.shape[0]
    tm = min(512, m)
    outs = pl.pallas_call(
        _rmsnorm_kernel,
        grid=(m // tm,),
        in_specs=[pl.BlockSpec((tm, d), lambda i: (i, 0)),
                  pl.BlockSpec((n, d), lambda i: (0, 0))],
        out_specs=[pl.BlockSpec((tm, d), lambda i: (i, 0)) for _ in range(n)],
        out_shape=[jax.ShapeDtypeStruct((m, d), out_dtype) for _ in range(n)],
        compiler_params=pltpu.CompilerParams(
            dimension_semantics=("arbitrary",), vmem_limit_bytes=VMEM_LIMIT),
        name="rmsnorm",
    )(x, gains)
    return outs


def _mm_kernel(a_ref, w_ref, x_ref, o_ref, *, w_is_nk, epilogue):
    w = w_ref[...].astype(BF16)
    if w_is_nk:
        acc = lax.dot_general(a_ref[...], w, _NT, preferred_element_type=F32)
    else:
        acc = jnp.dot(a_ref[...], w, preferred_element_type=F32)
    if epilogue == "residual":
        out = x_ref[...] + acc
    elif epilogue == "sigmoid":
        out = _sigmoid(acc)
    else:
        out = acc * x_ref[...]
    o_ref[...] = out.astype(o_ref.dtype)


def _matmul(a, w, *, out_dtype, scale=None, residual=None, sigmoid=False, n_cols=None, w_is_nk=False, name):
    m, k = a.shape
    n = (w.shape[0] if w_is_nk else w.shape[1]) if n_cols is None else n_cols
    tn = min(512, n)
    assert n % tn == 0
    if n // tn >= 32:
        tm = min(2048, m)
        a_spec = pl.BlockSpec((tm, k), lambda i, j: (i, 0), pipeline_mode=pl.Buffered(1))
    else:
        tm = min(1024, m)
        a_spec = pl.BlockSpec((tm, k), lambda i, j: (i, 0))
    w_spec = pl.BlockSpec((tn, k), lambda i, j: (j, 0)) if w_is_nk else pl.BlockSpec((k, tn), lambda i, j: (0, j))
    o_spec = pl.BlockSpec((tm, tn), lambda i, j: (i, j))
    epilogue = "residual" if residual is not None else ("sigmoid" if sigmoid else "scale")
    kern = functools.partial(_mm_kernel, w_is_nk=w_is_nk, epilogue=epilogue)
    if residual is None:
        extra, extra_spec = scale, pl.BlockSpec((1, tn), lambda i, j: (0, j))
    else:
        extra, extra_spec = residual, o_spec
    return pl.pallas_call(
        kern,
        grid=(m // tm, n // tn),
        in_specs=[a_spec, w_spec, extra_spec],
        out_specs=o_spec,
        out_shape=jax.ShapeDtypeStruct((m, n), out_dtype),
        compiler_params=pltpu.CompilerParams(
            dimension_semantics=("arbitrary", "arbitrary"), vmem_limit_bytes=VMEM_LIMIT),
        name=name,
    )(a, w, extra)


def _cmp_kernel(raw_ref, pa_ref, pb_ref, w1a_ref, w1b_ref, w2_ref, o_ref, raw32_ref):
    half = NSA_CMP_STRIDE
    nr = raw_ref.shape[0] // half
    raw32_ref[...] = raw_ref[...].astype(F32)
    r = jnp.concatenate([raw32_ref[pl.ds(l, nr, stride=half), :] for l in range(half)], axis=1)
    xa = (r + pa_ref[...]).astype(BF16)
    xb = (r + pb_ref[...]).astype(BF16)
    ya = jnp.dot(xa, w1a_ref[...], preferred_element_type=F32)
    yb = jnp.dot(xb, w1b_ref[...], preferred_element_type=F32)
    hid = ya + pltpu.roll(yb, nr - 1, 0)
    hid = hid * _sigmoid(hid)
    o_ref[...] = jnp.dot(hid.astype(BF16), w2_ref[...], preferred_element_type=F32).astype(o_ref.dtype)


def _compress(proj, col0, pa, pb, w1a, w1b, w2, *, batch, seq):
    g_n, dh = NSA_KV_HEADS, HEAD_DIM
    nr = seq // NSA_CMP_STRIDE
    kk = NSA_CMP_STRIDE * dh
    cb = col0 // dh
    sq = pl.Squeezed()
    wspec = lambda shape: pl.BlockSpec((sq,) + shape, lambda t, i: (t, 0, 0))
    return pl.pallas_call(
        _cmp_kernel,
        grid=(2, batch * g_n),
        in_specs=[pl.BlockSpec((seq, dh), lambda t, i: (i // g_n, cb + t * g_n + i % g_n)),
                  wspec((1, kk)), wspec((1, kk)), wspec((kk, dh)), wspec((kk, dh)), wspec((dh, dh))],
        out_specs=pl.BlockSpec((sq, sq, nr, dh), lambda t, i: (t, i, 0, 0)),
        out_shape=jax.ShapeDtypeStruct((2, batch * g_n, nr, dh), BF16),
        scratch_shapes=[pltpu.VMEM((seq, dh), F32)],
        compiler_params=pltpu.CompilerParams(
            dimension_semantics=("arbitrary", "arbitrary"), vmem_limit_bytes=VMEM_LIMIT),
        name="nsa_compress",
    )(proj, pa, pb, w1a, w1b, w2)


def _stack_heads(ref, n_heads, col0=0):
    return jnp.concatenate(
        [ref[:, col0 + h * HEAD_DIM:col0 + (h + 1) * HEAD_DIM] for h in range(n_heads)], axis=0)


def _query_in_tile(n_heads, tq, width):
    assert tq & (tq - 1) == 0, "query tile must be a power of two"
    return lax.broadcasted_iota(jnp.int32, (n_heads * tq, width), 0) & (tq - 1)


def _slope_extras(sl_ref, head0, n_heads, tq):
    lane = lax.broadcasted_iota(jnp.int32, (tq, LANES), 1)
    blocks = []
    for h in range(n_heads):
        x = jnp.zeros((tq, LANES), F32)
        for piece in range(3):
            hit = (lane == POS_COL + piece) | (lane == POS_COL + 3 + piece)
            x = jnp.where(hit, sl_ref[piece, head0 + h], x)
        blocks.append(x)
    return jnp.concatenate(blocks, axis=0)


def _causal_add(n_heads, tq, chunk, strict_future):
    qq = _query_in_tile(n_heads, tq, LANES)
    kk = lax.broadcasted_iota(jnp.int32, (n_heads * tq, LANES), 1) + chunk * LANES
    keep = (kk > qq) if strict_future else (kk <= qq)
    return jnp.where(keep, 0.0, NEG_INF)


def _positions_only(kx_tile):
    lane = lax.broadcasted_iota(jnp.int32, kx_tile.shape, 1)
    return jnp.where(lane >= POS_COL, kx_tile, jnp.zeros_like(kx_tile))


def _pad_rows_to_lanes(x):
    n = x.shape[0]
    assert n <= POS_COL, "block one-hot columns must stay clear of the slope/position columns"
    return jnp.concatenate([x, jnp.zeros((LANES - n, x.shape[1]), x.dtype)], axis=0)


def _online_tile(s_ref, p_ref, m_ref, l_ref, a_ref, acc_ref, v, *, rows, width, first):
    nch = width // LANES
    mt = s_ref[:, 0:LANES]
    for c in range(1, nch):
        mt = jnp.maximum(mt, s_ref[:, c * LANES:(c + 1) * LANES])
    mrow = jnp.max(mt, axis=-1, keepdims=True)
    if first:
        m_new = jnp.broadcast_to(mrow, (rows, LANES))
    else:
        m_old = m_ref[...]
        m_new = jnp.maximum(m_old, mrow)
        a_ref[...] = jnp.exp2(m_old - m_new)
    m_ref[...] = m_new
    ps = None
    for c in range(nch):
        cs = slice(c * LANES, (c + 1) * LANES)
        p = jnp.exp2(s_ref[:, cs] - m_new)
        p_ref[:, cs] = p.astype(BF16)
        ps = p if ps is None else ps + p
    if first:
        l_ref[...] = ps
    else:
        l_ref[...] = a_ref[...] * l_ref[...] + ps
    pv = jnp.dot(p_ref[:, 0:width], v, preferred_element_type=F32)
    if first:
        acc_ref[...] = pv
    else:
        acc_ref[...] = a_ref[...] * acc_ref[...] + pv


def _finish(l_ref, acc_ref):
    return acc_ref[...] * (1.0 / jnp.sum(l_ref[...], axis=-1, keepdims=True))


def _key_extras(pos, block=None):
    pos = pos[:, None]
    col = jnp.arange(LANES)[None, :]
    x = jnp.zeros((pos.shape[0], LANES), F32)
    if block is not None:
        x = jnp.where(col == pos // block, 1.0, x)
    x = jnp.where((col >= POS_COL) & (col < POS_COL + 3), (pos // POS_SPLIT) * POS_SPLIT, x)
    x = jnp.where((col >= POS_COL + 3) & (col < POS_COL + 6), pos % POS_SPLIT, x)
    return x.astype(BF16)


def _slope_pieces():
    s = 2.0 ** (-8.0 * jnp.arange(1, N_HEADS + 1, dtype=F32) / N_HEADS) * LOG2E
    a = s.astype(BF16).astype(F32)
    b = (s - a).astype(BF16).astype(F32)
    c = (s - a - b).astype(BF16).astype(F32)
    return jnp.stack([a, b, c])


def _nsa_kernel(sl_ref, q_ref, *refs, tq, tk, n_top, gps):
    z_refs = [refs[br * gps:(br + 1) * gps] for br in range(3)]
    (gt_ref, ks_ref, vs_ref, kw_ref, vw_ref, kc_ref, vc_ref, kx_ref, wx_ref, cx_ref, ovl_ref, gx_ref, o_ref,
     s_ref, p_ref, m_ref, l_ref, a_ref, acc_ref, sw_ref, pw_ref, mw_ref, lw_ref, accw_ref,
     ob_ref, used_ref) = refs[3 * gps:]
    gp = pl.program_id(1)
    i = pl.program_id(2)
    t0 = i * tq
    hpg = NSA_HPG
    gw = hpg * HEAD_DIM
    rows = hpg * tq
    grp = range(gps)
    kcol = lambda u: slice(u * HEAD_DIM, (u + 1) * HEAD_DIM)
    stats = lambda u: (s_ref.at[u], p_ref.at[u], m_ref.at[u], l_ref.at[u], a_ref.at[u], acc_ref.at[u])
    nqc = tq // LANES
    d0 = pl.multiple_of(t0, tq)
    qa = [_stack_heads(q_ref, hpg, u * gw) for u in grp]
    qx0 = [_slope_extras(sl_ref, (gp * gps + u) * hpg, hpg, tq).astype(BF16) for u in grp]
    q_aug0 = [jnp.concatenate([qa[u], qx0[u]], axis=1) for u in grp]

    nc = kc_ref.shape[1]
    cx = cx_ref[...]
    cend = lax.broadcasted_iota(jnp.int32, (1, nc), 1) * NSA_CMP_STRIDE + (NSA_CMP_LEN - 1)
    tqv = t0 + (lax.broadcasted_iota(jnp.int32, (rows, 1), 0) & (tq - 1))
    mask = cend <= tqv
    row_sees_block = tqv >= NSA_CMP_LEN - 1
    ovl = ovl_ref[...]
    nb = ovl.shape[0]
    j = lax.broadcasted_iota(jnp.int32, (nb, tq), 0)
    blkq = (t0 + lax.broadcasted_iota(jnp.int32, (nb, tq), 1)) >> (NSA_SEL_BLOCK.bit_length() - 1)
    forced = (j == 0) | (j == blkq) | (j == blkq - 1)
    blk0 = t0 >> (NSA_SEL_BLOCK.bit_length() - 1)
    q_aug = []
    any_sel = None
    for u in grp:
        kc_aug = jnp.concatenate([kc_ref[u], cx], axis=1)
        s = lax.dot_general(q_aug0[u], kc_aug, _NT, preferred_element_type=F32)
        s = jnp.where(mask, s, NEG_INF)
        mx = jnp.max(s, axis=-1, keepdims=True)
        e = jnp.exp2(s - mx)
        den = jnp.sum(e, axis=-1, keepdims=True)
        p = e * jnp.where(row_sees_block, 1.0 / den, 0.0)
        ob_ref[u, 0] = jnp.dot(p.astype(BF16), vc_ref[u], preferred_element_type=F32)
        psum = p[0:tq]
        for h in range(1, hpg):
            psum = psum + p[h * tq:(h + 1) * tq]
        p_hi = psum.astype(BF16)
        p_lo = (psum - p_hi.astype(F32)).astype(BF16)
        imp = (lax.dot_general(ovl, p_hi, _NT, preferred_element_type=F32)
               + lax.dot_general(ovl, p_lo, _NT, preferred_element_type=F32))
        imp = jnp.where(forced, FORCE_SCORE, jnp.where(j > blkq, NEG_INF, imp))
        rank = jnp.zeros((nb, tq), jnp.int32)
        for ii in range(nb):
            row = imp[ii:ii + 1, :]
            beats = (row > imp) | ((row == imp) & (j > ii))
            rank = rank + beats.astype(jnp.int32)
        selneg = jnp.where((rank < n_top) & (j < blk0), 0.0, NEG_INF).astype(F32)
        any_sel = selneg if any_sel is None else jnp.maximum(any_sel, selneg)
        selneg_q = _pad_rows_to_lanes(selneg).T.astype(BF16)
        qx = qx0[u] + jnp.concatenate([selneg_q] * hpg, axis=0)
        q_aug.append(jnp.concatenate([qa[u], qx], axis=1))
    bpt = tk // NSA_SEL_BLOCK
    for t in range(nb // bpt):
        used_ref[t] = (jnp.max(any_sel[t * bpt:(t + 1) * bpt, :]) > 0.5 * NEG_INF).astype(jnp.int32)

    causal_add = [_causal_add(hpg, tq, c, False) for c in range(nqc)]
    future_add = [_causal_add(hpg, tq, c, True) for c in range(nqc)]
    wlen = NSA_WINDOW + tq
    wxt = wx_ref[pl.ds(d0, wlen), :]
    for u in grp:
        su = sw_ref.at[u]
        kw_aug = jnp.concatenate([kw_ref[pl.ds(d0, wlen), kcol(u)], wxt], axis=1)
        su[...] = lax.dot_general(q_aug0[u], kw_aug, _NT, preferred_element_type=F32)
        for c in range(nqc):
            lo = slice(c * LANES, (c + 1) * LANES)
            su[:, lo] = su[:, lo] + future_add[c]
            hi = slice(NSA_WINDOW + c * LANES, NSA_WINDOW + (c + 1) * LANES)
            su[:, hi] = su[:, hi] + causal_add[c]
    for u in grp:
        _online_tile(sw_ref.at[u], pw_ref.at[u], mw_ref.at[u], lw_ref.at[u], None, accw_ref.at[u],
                     vw_ref[pl.ds(d0, wlen), kcol(u)], rows=rows, width=wlen, first=True)
        ob_ref[u, 2] = _finish(lw_ref.at[u], accw_ref.at[u])

    kxd = _positions_only(kx_ref[pl.ds(d0, tq), :])
    for u in grp:
        su = s_ref.at[u]
        kd_aug = jnp.concatenate([ks_ref[pl.ds(d0, tq), kcol(u)], kxd], axis=1)
        su[:, 0:tq] = lax.dot_general(q_aug0[u], kd_aug, _NT, preferred_element_type=F32)
        for c in range(nqc):
            cs = slice(c * LANES, (c + 1) * LANES)
            su[:, cs] = su[:, cs] + causal_add[c]
    for u in grp:
        _online_tile(*stats(u), vs_ref[pl.ds(d0, tq), kcol(u)], rows=rows, width=tq, first=True)

    def sel_body(kt, carry):
        @pl.when(used_ref[kt] != 0)
        def _():
            k0 = pl.multiple_of(kt * tk, tk)
            kxt = kx_ref[pl.ds(k0, tk), :]
            for u in grp:
                k_aug = jnp.concatenate([ks_ref[pl.ds(k0, tk), kcol(u)], kxt], axis=1)
                s_ref[u, :, 0:tk] = lax.dot_general(q_aug[u], k_aug, _NT, preferred_element_type=F32)
            for u in grp:
                _online_tile(*stats(u), vs_ref[pl.ds(k0, tk), kcol(u)], rows=rows, width=tk, first=False)
        return carry

    lax.fori_loop(0, (t0 + tk - 1) // tk, sel_body, 0)
    for u in grp:
        ob_ref[u, 1] = _finish(l_ref.at[u], acc_ref.at[u])

    gates = gt_ref[...].astype(F32)
    for u in grp:
        gts = pltpu.roll(gates, (LANES - (gp * gps + u) * hpg) % LANES, 1)
        gfull = jnp.dot(gts.astype(BF16), gx_ref[...], preferred_element_type=F32)
        for h in range(hpg):
            rs = slice(h * tq, (h + 1) * tq)
            cs = slice(h * HEAD_DIM, (h + 1) * HEAD_DIM)
            mix = jnp.zeros((tq, HEAD_DIM), F32)
            for br in range(3):
                z = z_refs[br][u][:, cs].astype(F32)
                gate = gfull[:, (br * hpg + h) * LANES:(br * hpg + h + 1) * LANES]
                mix = mix + gate * ob_ref[u, br, rs, :] * (z * _sigmoid(z))
            o_ref[:, u * gw + h * HEAD_DIM:u * gw + (h + 1) * HEAD_DIM] = mix.astype(o_ref.dtype)


def _nsa_attention(pieces, proj, gates, win, kvc, ovl_t, *, batch, seq):
    m = proj.shape[0]
    tq = min(128, seq)
    tk = min(512, seq)
    nt = seq // tq
    nb = seq // NSA_SEL_BLOCK
    nc = seq // NSA_CMP_STRIDE
    hpg, g_n, dh = NSA_HPG, NSA_KV_HEADS, HEAD_DIM
    gps = NSA_GROUPS_PER_STEP
    ngp = g_n // gps
    gw = hpg * dh
    q_cols = N_HEADS * dh
    kv_col0 = q_cols // (gps * dh)
    z_col0 = (q_cols + 3 * 2 * g_n * dh) // gw
    rows = hpg * tq
    wlen = NSA_WINDOW + tq
    sq = pl.Squeezed()
    assert nb <= POS_COL and g_n % gps == 0
    kx = _key_extras(jnp.arange(seq), NSA_SEL_BLOCK)
    wx = _key_extras(jnp.arange(seq + NSA_WINDOW))
    cx = _key_extras(jnp.arange(nc) * NSA_CMP_STRIDE + (NSA_CMP_LEN - 1))
    gsrc = (jnp.arange(3 * hpg) // hpg) * N_HEADS + jnp.arange(3 * hpg) % hpg
    gx = (jnp.arange(LANES)[:, None] == jnp.repeat(gsrc, LANES)[None, :]).astype(BF16)

    once = pl.Buffered(1)

    def kv_spec(branch, kv):
        c = kv_col0 + (branch * 2 + kv) * ngp
        return pl.BlockSpec((seq, gps * dh), lambda b, g, i: (b, c + g), pipeline_mode=once)

    def z_spec(branch, u):
        c = z_col0 + branch * g_n + u
        return pl.BlockSpec((tq, gw), lambda b, g, i: (b * nt + i, c + g * gps))

    win_spec = lambda kv: pl.BlockSpec((sq, seq + NSA_WINDOW, gps * dh), lambda b, g, i: (b, 0, kv * ngp + g),
                                       pipeline_mode=once)
    cmp_spec = lambda kv: pl.BlockSpec((sq, gps, nc, dh), lambda b, g, i: (kv, b * ngp + g, 0, 0))
    full = lambda arr: pl.BlockSpec(arr.shape, lambda b, g, i: (0,) * arr.ndim, pipeline_mode=once)
    in_specs = (
        [pl.BlockSpec(memory_space=pltpu.SMEM),
         pl.BlockSpec((tq, gps * gw), lambda b, g, i: (b * nt + i, g))]
        + [z_spec(br, u) for br in range(3) for u in range(gps)]
        + [pl.BlockSpec((tq, LANES), lambda b, g, i: (b * nt + i, 0)),
           kv_spec(1, 0), kv_spec(1, 1),
           win_spec(0), win_spec(1),
           cmp_spec(0), cmp_spec(1),
           full(kx), full(wx), full(cx), full(ovl_t), full(gx)])
    kern = functools.partial(_nsa_kernel, tq=tq, tk=tk, n_top=min(NSA_SEL_TOPN, nb), gps=gps)
    return pl.pallas_call(
        kern,
        grid=(batch, ngp, nt),
        in_specs=in_specs,
        out_specs=pl.BlockSpec((tq, gps * gw), lambda b, g, i: (b * nt + i, g)),
        out_shape=jax.ShapeDtypeStruct((m, q_cols), BF16),
        scratch_shapes=[pltpu.VMEM((gps, rows, tk), F32),
                        pltpu.VMEM((gps, rows, tk), BF16),
                        pltpu.VMEM((gps, rows, LANES), F32),
                        pltpu.VMEM((gps, rows, LANES), F32),
                        pltpu.VMEM((gps, rows, LANES), F32),
                        pltpu.VMEM((gps, rows, dh), F32),
                        pltpu.VMEM((gps, rows, wlen), F32),
                        pltpu.VMEM((gps, rows, wlen), BF16),
                        pltpu.VMEM((gps, rows, LANES), F32),
                        pltpu.VMEM((gps, rows, LANES), F32),
                        pltpu.VMEM((gps, rows, dh), F32),
                        pltpu.VMEM((gps, 3, rows, dh), F32),
                        pltpu.SMEM((seq // tk,), jnp.int32)],
        compiler_params=pltpu.CompilerParams(
            dimension_semantics=("arbitrary", "arbitrary", "arbitrary"), vmem_limit_bytes=VMEM_LIMIT),
        name="nsa_attention",
    )(pieces, proj, *([proj] * (3 * gps)), gates, proj, proj, win, win, kvc, kvc, kx, wx, cx, ovl_t, gx)


def _moba_kernel(sl_ref, q_ref, z_ref, k_ref, v_ref, mx_ref, o_ref,
                 km_ref, s_ref, p_ref, m_ref, l_ref, a_ref, acc_ref, *, tq, tk, n_top, gps):
    gp = pl.program_id(1)
    i = pl.program_id(2)
    hpg = MOBA_HPG
    gw = hpg * HEAD_DIM
    rows = hpg * tq
    seq = k_ref.shape[0]
    nbm = seq // MOBA_BLOCK
    bpt = tk // MOBA_BLOCK
    grp = range(gps)
    kcol = lambda u: slice(u * HEAD_DIM, (u + 1) * HEAD_DIM)
    stats = lambda u: (s_ref.at[u], p_ref.at[u], m_ref.at[u], l_ref.at[u], a_ref.at[u], acc_ref.at[u])

    @pl.when(i == 0)
    def _():
        blk = lax.broadcasted_iota(jnp.int32, (nbm, seq), 1) >> (MOBA_BLOCK.bit_length() - 1)
        avg = jnp.where(blk == lax.broadcasted_iota(jnp.int32, (nbm, seq), 0), 1.0 / MOBA_BLOCK, 0.0)
        for u in grp:
            km_ref[u] = jnp.dot(avg.astype(BF16), k_ref[:, kcol(u)], preferred_element_type=F32)

    n_io = lax.broadcasted_iota(jnp.int32, (nbm, rows), 0)
    past = n_io < i
    q_aug0, q_aug = [], []
    for u in grp:
        qa = _stack_heads(q_ref, hpg, u * gw)
        qx0 = _slope_extras(sl_ref, (gp * gps + u) * hpg, hpg, tq).astype(BF16)
        km = km_ref[u]
        km_hi = km.astype(BF16)
        km_lo = (km - km_hi.astype(F32)).astype(BF16)
        sb = (lax.dot_general(km_hi, qa, _NT, preferred_element_type=F32)
              + lax.dot_general(km_lo, qa, _NT, preferred_element_type=F32))
        sb = jnp.where(past, sb, NEG_INF)
        rank = jnp.zeros((nbm, rows), jnp.int32)
        for mm in range(nbm):
            row = sb[mm:mm + 1, :]
            beats = (row > sb) | ((row == sb) & (n_io > mm))
            rank = rank + beats.astype(jnp.int32)
        selneg = jnp.where(past & (rank < n_top), 0.0, NEG_INF).astype(F32)
        qx = qx0 + _pad_rows_to_lanes(selneg).T.astype(BF16)
        q_aug0.append(jnp.concatenate([qa, qx0], axis=1))
        q_aug.append(jnp.concatenate([qa, qx], axis=1))

    d0 = pl.multiple_of(i * tq, tq)
    mxd = _positions_only(mx_ref[pl.ds(d0, tq), :])
    causal_add = [_causal_add(hpg, tq, c, False) for c in range(tq // LANES)]
    for u in grp:
        su = s_ref.at[u]
        kd_aug = jnp.concatenate([k_ref[pl.ds(d0, tq), kcol(u)], mxd], axis=1)
        su[:, 0:tq] = lax.dot_general(q_aug0[u], kd_aug, _NT, preferred_element_type=F32)
        for c in range(tq // LANES):
            cs = slice(c * LANES, (c + 1) * LANES)
            su[:, cs] = su[:, cs] + causal_add[c]
    for u in grp:
        _online_tile(*stats(u), v_ref[pl.ds(d0, tq), kcol(u)], rows=rows, width=tq, first=True)

    def body(t, carry):
        k0 = pl.multiple_of(t * tk, tk)
        mxt = mx_ref[pl.ds(k0, tk), :]
        for u in grp:
            k_aug = jnp.concatenate([k_ref[pl.ds(k0, tk), kcol(u)], mxt], axis=1)
            s_ref[u] = lax.dot_general(q_aug[u], k_aug, _NT, preferred_element_type=F32)
        for u in grp:
            _online_tile(*stats(u), v_ref[pl.ds(k0, tk), kcol(u)], rows=rows, width=tk, first=False)
        return carry

    lax.fori_loop(0, (i + bpt - 1) // bpt, body, 0)
    for u in grp:
        o = _finish(l_ref.at[u], acc_ref.at[u])
        for h in range(hpg):
            cs = slice(u * gw + h * HEAD_DIM, u * gw + (h + 1) * HEAD_DIM)
            z = z_ref[:, cs].astype(F32)
            o_ref[:, cs] = (o[h * tq:(h + 1) * tq] * (z * _sigmoid(z))).astype(o_ref.dtype)


def _moba_attention(pieces, qz, kv, *, batch, seq):
    m = qz.shape[0]
    tq = MOBA_BLOCK
    tk = min(2 * MOBA_BLOCK, seq)
    nt = seq // tq
    hpg, g_n, dh = MOBA_HPG, MOBA_KV_HEADS, HEAD_DIM
    gps = MOBA_GROUPS_PER_STEP
    ngp = g_n // gps
    gw = gps * hpg * dh
    rows = hpg * tq
    nbm = seq // MOBA_BLOCK
    assert nbm <= POS_COL and g_n % gps == 0
    mx = _key_extras(jnp.arange(seq), MOBA_BLOCK)
    in_specs = [
        pl.BlockSpec(memory_space=pltpu.SMEM),
        pl.BlockSpec((tq, gw), lambda b, g, i: (b * nt + i, g)),
        pl.BlockSpec((tq, gw), lambda b, g, i: (b * nt + i, ngp + g)),
        pl.BlockSpec((seq, gps * dh), lambda b, g, i: (b, g)),
        pl.BlockSpec((seq, gps * dh), lambda b, g, i: (b, ngp + g)),
        pl.BlockSpec(mx.shape, lambda b, g, i: (0, 0)),
    ]
    kern = functools.partial(_moba_kernel, tq=tq, tk=tk, n_top=min(MOBA_TOPK, nbm), gps=gps)
    return pl.pallas_call(
        kern,
        grid=(batch, ngp, nt),
        in_specs=in_specs,
        out_specs=pl.BlockSpec((tq, gw), lambda b, g, i: (b * nt + i, g)),
        out_shape=jax.ShapeDtypeStruct((m, N_HEADS * dh), BF16),
        scratch_shapes=[pltpu.VMEM((gps, nbm, dh), F32),
                        pltpu.VMEM((gps, rows, tk), F32),
                        pltpu.VMEM((gps, rows, tk), BF16),
                        pltpu.VMEM((gps, rows, LANES), F32),
                        pltpu.VMEM((gps, rows, LANES), F32),
                        pltpu.VMEM((gps, rows, LANES), F32),
                        pltpu.VMEM((gps, rows, dh), F32)],
        compiler_params=pltpu.CompilerParams(
            dimension_semantics=("arbitrary", "arbitrary", "arbitrary"), vmem_limit_bytes=VMEM_LIMIT),
        name="moba_attention",
    )(pieces, qz, qz, kv, kv, mx)


def _nsa_layer(h, norm_g, w_in, pos_k, pos_v, w1_k, w2_k, w1_v, w2_v, w_out, *, batch, seq):
    dh, g_n = HEAD_DIM, NSA_KV_HEADS
    q_cols = N_HEADS * dh
    kv_cols = 3 * 2 * g_n * dh
    n_main = q_cols + kv_cols + 3 * q_cols
    n_gate = w_in.shape[1] - n_main
    assert n_gate == 3 * N_HEADS <= LANES and n_main % LANES == 0
    (xn,) = _rmsnorm(h, norm_g[None, :], BF16)
    scale = jnp.where(jnp.arange(n_main) < q_cols, dh ** -0.5 * LOG2E, 1.0).astype(F32)[None, :]
    w_t = w_in.T
    proj = _matmul(xn, w_t, out_dtype=BF16, scale=scale, n_cols=n_main, w_is_nk=True,
                   name="nsa_in_proj")
    w_gate_t = jnp.pad(w_t[n_main:], ((0, LANES - n_gate), (0, 0)))
    gates = _matmul(xn, w_gate_t, out_dtype=BF16, scale=jnp.ones((1, LANES), F32), sigmoid=True,
                    w_is_nk=True, name="nsa_gate_proj")

    half = NSA_CMP_STRIDE
    nr = seq // half
    pos = jnp.stack([pos_k, pos_v])
    pa = pos[:, :half].reshape(2, 1, half * dh)
    pb = pos[:, half:].reshape(2, 1, half * dh)
    w1 = jnp.stack([w1_k, w1_v])
    w1a = w1[:, :half].reshape(2, half * dh, dh).astype(BF16)
    w1b = w1[:, half:].reshape(2, half * dh, dh).astype(BF16)
    w2 = jnp.stack([w2_k, w2_v]).astype(BF16)
    kvc = _compress(proj, q_cols, pa, pb, w1a, w1b, w2, batch=batch, seq=seq)

    wcol = q_cols + 2 * 2 * g_n * dh
    win = proj[:, wcol:wcol + 2 * g_n * dh].reshape(batch, seq, 2 * g_n * dh)
    win = jnp.pad(win, ((0, 0), (NSA_WINDOW, 0), (0, 0)))

    nb = seq // NSA_SEL_BLOCK
    cstart = jnp.arange(nr)[None, :] * NSA_CMP_STRIDE
    sstart = jnp.arange(nb)[:, None] * NSA_SEL_BLOCK
    ovl_t = ((cstart <= sstart + NSA_SEL_BLOCK - 1)
             & (cstart + NSA_CMP_LEN - 1 >= sstart)
             & (jnp.arange(nr)[None, :] < (seq - NSA_CMP_LEN) // NSA_CMP_STRIDE + 1)).astype(BF16)
    mix = _nsa_attention(_slope_pieces(), proj, gates, win, kvc, ovl_t, batch=batch, seq=seq)
    return _matmul(mix, w_out, out_dtype=F32, residual=h, name="nsa_out_proj")


def _moba_shared_kv(xn_kv, kv_w):
    one = jnp.ones((1, kv_w.shape[1]), F32)
    return _matmul(xn_kv, kv_w, out_dtype=BF16, scale=one, name="moba_kv_proj")


def _moba_layer(h, xn_q, kv, w_in, w_out, *, batch, seq):
    dh = HEAD_DIM
    q_cols = N_HEADS * dh
    scale = jnp.where(jnp.arange(w_in.shape[1]) < q_cols, dh ** -0.5 * LOG2E, 1.0).astype(F32)[None, :]
    qz = _matmul(xn_q, w_in, out_dtype=BF16, scale=scale, name="moba_in_proj")
    o = _moba_attention(_slope_pieces(), qz, kv, batch=batch, seq=seq)
    return _matmul(o, w_out, out_dtype=F32, residual=h, name="moba_out_proj")


def kernel(x, a_norm_g, a_w_in, a_cmp_pos_k, a_cmp_pos_v, a_cmp_w1_k, a_cmp_w2_k, a_cmp_w1_v, a_cmp_w2_v,
           a_w_out, kv_norm_g, kv_w, b_norm_g, b_w_in, b_w_out, final_norm_g):
    batch, seq, d = x.shape
    h = x.reshape(batch * seq, d)
    for layer in range(a_norm_g.shape[0]):
        h = _nsa_layer(h, a_norm_g[layer], a_w_in[layer], a_cmp_pos_k[layer], a_cmp_pos_v[layer],
                       a_cmp_w1_k[layer], a_cmp_w2_k[layer], a_cmp_w1_v[layer], a_cmp_w2_v[layer],
                       a_w_out[layer], batch=batch, seq=seq)
    kv = None
    for layer in range(b_norm_g.shape[0]):
        if layer == 0:
            xn_kv, xn_q = _rmsnorm(h, jnp.stack([kv_norm_g, b_norm_g[layer]]), BF16)
            kv = _moba_shared_kv(xn_kv, kv_w)
        else:
            (xn_q,) = _rmsnorm(h, b_norm_g[layer][None, :], BF16)
        h = _moba_layer(h, xn_q, kv, b_w_in[layer], b_w_out[layer], batch=batch, seq=seq)
    (out,) = _rmsnorm(h, final_norm_g[None, :], F32)
    return out.reshape(batch, seq, d)
```

```python
import functools

import jax
import jax.numpy as jnp
from jax import lax
from jax.experimental import pallas as pl
from jax.experimental.pallas import tpu as pltpu

F32 = jnp.float32
BF16 = jnp.bfloat16

N_HEADS = 32
HEAD_DIM = 128
NSA_KV_HEADS = 4
NSA_HPG = N_HEADS // NSA_KV_HEADS
NSA_CMP_LEN = 32
NSA_CMP_STRIDE = 16
NSA_SEL_BLOCK = 64
NSA_SEL_TOPN = 16
NSA_WINDOW = 512
MOBA_KV_HEADS = 8
MOBA_HPG = N_HEADS // MOBA_KV_HEADS
MOBA_BLOCK = 256
MOBA_TOPK = 3
RMS_EPS = 1e-6
NEG_INF = -1e30
FORCE_SCORE = 1e9
LOG2E = 1.4426950408889634

LANES = 128
VMEM_LIMIT = 56 * 1024 * 1024
NSA_GROUPS_PER_STEP = 2
MOBA_GROUPS_PER_STEP = 4
POS_COL = 120
POS_SPLIT = 64

_NT = (((1,), (1,)), ((), ()))


def _sigmoid(x):
    return 1.0 / (1.0 + jnp.exp(-x))


def _silu(x):
    h = 0.5 * x
    return h + h * jnp.tanh(h)


def _rmsnorm_kernel(x_ref, g_ref, *o_refs):
    x = x_ref[...].astype(F32)
    y = x * lax.rsqrt(jnp.mean(x * x, axis=-1, keepdims=True) + RMS_EPS)
    for i, o_ref in enumerate(o_refs):
        o_ref[...] = (y * g_ref[i:i + 1, :]).astype(o_ref.dtype)


def _rmsnorm(x, gains, out_dtype):
    m, d = x.shape
    n = gains.shape[0]
    tm = min(512, m)
    outs = pl.pallas_call(
        _rmsnorm_kernel,
        grid=(m // tm,),
        in_specs=[pl.BlockSpec((tm, d), lambda i: (i, 0)),
                  pl.BlockSpec((n, d), lambda i: (0, 0))],
        out_specs=[pl.BlockSpec((tm, d), lambda i: (i, 0)) for _ in range(n)],
        out_shape=[jax.ShapeDtypeStruct((m, d), out_dtype) for _ in range(n)],
        compiler_params=pltpu.CompilerParams(
            dimension_semantics=("arbitrary",), vmem_limit_bytes=VMEM_LIMIT),
        name="rmsnorm",
    )(x, gains)
    return outs


def _mm_kernel(a_ref, w_ref, x_ref, o_ref, *, w_is_nk, epilogue):
    w = w_ref[...].astype(BF16)
    if w_is_nk:
        acc = lax.dot_general(a_ref[...], w, _NT, preferred_element_type=F32)
    else:
        acc = jnp.dot(a_ref[...], w, preferred_element_type=F32)
    if epilogue == "residual":
        out = x_ref[...] + acc
    elif epilogue == "sigmoid":
        out = _sigmoid(acc)
    else:
        out = acc * x_ref[...]
    o_ref[...] = out.astype(o_ref.dtype)


def _matmul(a, w, *, out_dtype, scale=None, residual=None, sigmoid=False, n_cols=None, w_is_nk=False, name):
    m, k = a.shape
    n = (w.shape[0] if w_is_nk else w.shape[1]) if n_cols is None else n_cols
    tn = min(512, n)
    assert n % tn == 0
    if n // tn >= 32:
        tm = min(2048, m)
        a_spec = pl.BlockSpec((tm, k), lambda i, j: (i, 0), pipeline_mode=pl.Buffered(1))
    else:
        tm = min(1024, m)
        a_spec = pl.BlockSpec((tm, k), lambda i, j: (i, 0))
    w_spec = pl.BlockSpec((tn, k), lambda i, j: (j, 0)) if w_is_nk else pl.BlockSpec((k, tn), lambda i, j: (0, j))
    o_spec = pl.BlockSpec((tm, tn), lambda i, j: (i, j))
    epilogue = "residual" if residual is not None else ("sigmoid" if sigmoid else "scale")
    kern = functools.partial(_mm_kernel, w_is_nk=w_is_nk, epilogue=epilogue)
    if residual is None:
        extra, extra_spec = scale, pl.BlockSpec((1, tn), lambda i, j: (0, j))
    else:
        extra, extra_spec = residual, o_spec
    return pl.pallas_call(
        kern,
        grid=(m // tm, n // tn),
        in_specs=[a_spec, w_spec, extra_spec],
        out_specs=o_spec,
        out_shape=jax.ShapeDtypeStruct((m, n), out_dtype),
        compiler_params=pltpu.CompilerParams(
            dimension_semantics=("arbitrary", "arbitrary"), vmem_limit_bytes=VMEM_LIMIT),
        name=name,
    )(a, w, extra)


def _cmp_kernel(raw_ref, pa_ref, pb_ref, w1a_ref, w1b_ref, w2_ref, o_ref, raw32_ref):
    half = NSA_CMP_STRIDE
    nr = raw_ref.shape[0] // half
    raw32_ref[...] = raw_ref[...].astype(F32)
    r = jnp.concatenate([raw32_ref[pl.ds(l, nr, stride=half), :] for l in range(half)], axis=1)
    xa = (r + pa_ref[...]).astype(BF16)
    xb = (r + pb_ref[...]).astype(BF16)
    ya = jnp.dot(xa, w1a_ref[...], preferred_element_type=F32)
    yb = jnp.dot(xb, w1b_ref[...], preferred_element_type=F32)
    hid = ya + pltpu.roll(yb, nr - 1, 0)
    hid = hid * _sigmoid(hid)
    o_ref[...] = jnp.dot(hid.astype(BF16), w2_ref[...], preferred_element_type=F32).astype(o_ref.dtype)


def _compress(proj, col0, pa, pb, w1a, w1b, w2, *, batch, seq):
    g_n, dh = NSA_KV_HEADS, HEAD_DIM
    nr = seq // NSA_CMP_STRIDE
    kk = NSA_CMP_STRIDE * dh
    cb = col0 // dh
    sq = pl.Squeezed()
    wspec = lambda shape: pl.BlockSpec((sq,) + shape, lambda t, i: (t, 0, 0))
    return pl.pallas_call(
        _cmp_kernel,
        grid=(2, batch * g_n),
        in_specs=[pl.BlockSpec((seq, dh), lambda t, i: (i // g_n, cb + t * g_n + i % g_n)),
                  wspec((1, kk)), wspec((1, kk)), wspec((kk, dh)), wspec((kk, dh)), wspec((dh, dh))],
        out_specs=pl.BlockSpec((sq, sq, nr, dh), lambda t, i: (t, i, 0, 0)),
        out_shape=jax.ShapeDtypeStruct((2, batch * g_n, nr, dh), BF16),
        scratch_shapes=[pltpu.VMEM((seq, dh), F32)],
        compiler_params=pltpu.CompilerParams(
            dimension_semantics=("arbitrary", "arbitrary"), vmem_limit_bytes=VMEM_LIMIT),
        name="nsa_compress",
    )(proj, pa, pb, w1a, w1b, w2)


def _stack_heads(ref, n_heads, col0=0):
    return jnp.concatenate(
        [ref[:, col0 + h * HEAD_DIM:col0 + (h + 1) * HEAD_DIM] for h in range(n_heads)], axis=0)


def _query_in_tile(n_heads, tq, width):
    assert tq & (tq - 1) == 0, "query tile must be a power of two"
    return lax.broadcasted_iota(jnp.int32, (n_heads * tq, width), 0) & (tq - 1)


def _slope_extras(sl_ref, head0, n_heads, tq):
    lane = lax.broadcasted_iota(jnp.int32, (tq, LANES), 1)
    blocks = []
    for h in range(n_heads):
        x = jnp.zeros((tq, LANES), F32)
        for piece in range(3):
            hit = (lane == POS_COL + piece) | (lane == POS_COL + 3 + piece)
            x = jnp.where(hit, sl_ref[piece, head0 + h], x)
        blocks.append(x)
    return jnp.concatenate(blocks, axis=0)


def _causal_add(n_heads, tq, chunk, strict_future):
    qq = _query_in_tile(n_heads, tq, LANES)
    kk = lax.broadcasted_iota(jnp.int32, (n_heads * tq, LANES), 1) + chunk * LANES
    keep = (kk > qq) if strict_future else (kk <= qq)
    return jnp.where(keep, 0.0, NEG_INF)


def _positions_only(kx_tile):
    lane = lax.broadcasted_iota(jnp.int32, kx_tile.shape, 1)
    return jnp.where(lane >= POS_COL, kx_tile, jnp.zeros_like(kx_tile))


def _pad_rows_to_lanes(x):
    n = x.shape[0]
    assert n <= POS_COL, "block one-hot columns must stay clear of the slope/position columns"
    return jnp.concatenate([x, jnp.zeros((LANES - n, x.shape[1]), x.dtype)], axis=0)


def _online_tile(s_ref, p_ref, m_ref, l_ref, a_ref, acc_ref, v, *, rows, width, first):
    nch = width // LANES
    mt = s_ref[:, 0:LANES]
    for c in range(1, nch):
        mt = jnp.maximum(mt, s_ref[:, c * LANES:(c + 1) * LANES])
    mrow = jnp.max(mt, axis=-1, keepdims=True)
    if first:
        m_new = jnp.broadcast_to(mrow, (rows, LANES))
    else:
        m_old = m_ref[...]
        m_new = jnp.maximum(m_old, mrow)
        a_ref[...] = jnp.exp2(m_old - m_new)
    m_ref[...] = m_new
    ps = None
    for c in range(nch):
        cs = slice(c * LANES, (c + 1) * LANES)
        p = jnp.exp2(s_ref[:, cs] - m_new)
        p_ref[:, cs] = p.astype(BF16)
        ps = p if ps is None else ps + p
    if first:
        l_ref[...] = ps
    else:
        l_ref[...] = a_ref[...] * l_ref[...] + ps
    pv = jnp.dot(p_ref[:, 0:width], v, preferred_element_type=F32)
    if first:
        acc_ref[...] = pv
    else:
        acc_ref[...] = a_ref[...] * acc_ref[...] + pv


def _finish(l_ref, acc_ref):
    return acc_ref[...] * (1.0 / jnp.sum(l_ref[...], axis=-1, keepdims=True))


def _key_extras(pos, block=None):
    pos = pos[:, None]
    col = jnp.arange(LANES)[None, :]
    x = jnp.zeros((pos.shape[0], LANES), F32)
    if block is not None:
        x = jnp.where(col == pos // block, 1.0, x)
    x = jnp.where((col >= POS_COL) & (col < POS_COL + 3), (pos // POS_SPLIT) * POS_SPLIT, x)
    x = jnp.where((col >= POS_COL + 3) & (col < POS_COL + 6), pos % POS_SPLIT, x)
    return x.astype(BF16)


def _slope_pieces():
    s = 2.0 ** (-8.0 * jnp.arange(1, N_HEADS + 1, dtype=F32) / N_HEADS) * LOG2E
    a = s.astype(BF16).astype(F32)
    b = (s - a).astype(BF16).astype(F32)
    c = (s - a - b).astype(BF16).astype(F32)
    return jnp.stack([a, b, c])


def _nsa_kernel(sl_ref, q_ref, *refs, tq, tk, n_top, gps):
    z_refs = [refs[br * gps:(br + 1) * gps] for br in range(3)]
    (gt_ref, ks_ref, vs_ref, kw_ref, vw_ref, kc_ref, vc_ref, kx_ref, wx_ref, cx_ref, ovl_ref, gx_ref, o_ref,
     s_ref, p_ref, m_ref, l_ref, a_ref, acc_ref, sw_ref, pw_ref, mw_ref, lw_ref, accw_ref,
     ob_ref, used_ref) = refs[3 * gps:]
    gp = pl.program_id(1)
    i = pl.program_id(2)
    t0 = i * tq
    hpg = NSA_HPG
    gw = hpg * HEAD_DIM
    rows = hpg * tq
    grp = range(gps)
    kcol = lambda u: slice(u * HEAD_DIM, (u + 1) * HEAD_DIM)
    stats = lambda u: (s_ref.at[u], p_ref.at[u], m_ref.at[u], l_ref.at[u], a_ref.at[u], acc_ref.at[u])
    nqc = tq // LANES
    d0 = pl.multiple_of(t0, tq)
    qa = [_stack_heads(q_ref, hpg, u * gw) for u in grp]
    qx0 = [_slope_extras(sl_ref, (gp * gps + u) * hpg, hpg, tq).astype(BF16) for u in grp]
    q_aug0 = [jnp.concatenate([qa[u], qx0[u]], axis=1) for u in grp]

    nc = kc_ref.shape[1]
    cx = cx_ref[...]
    cend = lax.broadcasted_iota(jnp.int32, (1, nc), 1) * NSA_CMP_STRIDE + (NSA_CMP_LEN - 1)
    tqv = t0 + (lax.broadcasted_iota(jnp.int32, (rows, 1), 0) & (tq - 1))
    mask = cend <= tqv
    row_sees_block = tqv >= NSA_CMP_LEN - 1
    ovl = ovl_ref[...]
    nb = ovl.shape[0]
    j = lax.broadcasted_iota(jnp.int32, (nb, tq), 0)
    blkq = (t0 + lax.broadcasted_iota(jnp.int32, (nb, tq), 1)) >> (NSA_SEL_BLOCK.bit_length() - 1)
    forced = (j == 0) | (j == blkq) | (j == blkq - 1)
    blk0 = t0 >> (NSA_SEL_BLOCK.bit_length() - 1)
    q_aug = []
    any_sel = None
    for u in grp:
        kc_aug = jnp.concatenate([kc_ref[u], cx], axis=1)
        s = lax.dot_general(q_aug0[u], kc_aug, _NT, preferred_element_type=F32)
        s = jnp.where(mask, s, NEG_INF)
        mx = jnp.max(s, axis=-1, keepdims=True)
        e = jnp.exp2(s - mx)
        den = jnp.sum(e, axis=-1, keepdims=True)
        p = e * jnp.where(row_sees_block, 1.0 / den, 0.0)
        ob_ref[u, 0] = jnp.dot(p.astype(BF16), vc_ref[u], preferred_element_type=F32)
        psum = p[0:tq]
        for h in range(1, hpg):
            psum = psum + p[h * tq:(h + 1) * tq]
        p_hi = psum.astype(BF16)
        p_lo = (psum - p_hi.astype(F32)).astype(BF16)
        imp = (lax.dot_general(ovl, p_hi, _NT, preferred_element_type=F32)
               + lax.dot_general(ovl, p_lo, _NT, preferred_element_type=F32))
        imp = jnp.where(forced, FORCE_SCORE, jnp.where(j > blkq, NEG_INF, imp))
        rank = jnp.zeros((nb, tq), jnp.int32)
        for ii in range(nb):
            row = imp[ii:ii + 1, :]
            beats = (row > imp) | ((row == imp) & (j > ii))
            rank = rank + beats.astype(jnp.int32)
        selneg = jnp.where((rank < n_top) & (j < blk0), 0.0, NEG_INF).astype(F32)
        any_sel = selneg if any_sel is None else jnp.maximum(any_sel, selneg)
        selneg_q = _pad_rows_to_lanes(selneg).T.astype(BF16)
        qx = qx0[u] + jnp.concatenate([selneg_q] * hpg, axis=0)
        q_aug.append(jnp.concatenate([qa[u], qx], axis=1))
    bpt = tk // NSA_SEL_BLOCK
    for t in range(nb // bpt):
        used_ref[t] = (jnp.max(any_sel[t * bpt:(t + 1) * bpt, :]) > 0.5 * NEG_INF).astype(jnp.int32)

    causal_add = [_causal_add(hpg, tq, c, False) for c in range(nqc)]
    future_add = [_causal_add(hpg, tq, c, True) for c in range(nqc)]
    wlen = NSA_WINDOW + tq
    wxt = wx_ref[pl.ds(d0, wlen), :]
    for u in grp:
        su = sw_ref.at[u]
        kw_aug = jnp.concatenate([kw_ref[pl.ds(d0, wlen), kcol(u)], wxt], axis=1)
        su[...] = lax.dot_general(q_aug0[u], kw_aug, _NT, preferred_element_type=F32)
        for c in range(nqc):
            lo = slice(c * LANES, (c + 1) * LANES)
            su[:, lo] = su[:, lo] + future_add[c]
            hi = slice(NSA_WINDOW + c * LANES, NSA_WINDOW + (c + 1) * LANES)
            su[:, hi] = su[:, hi] + causal_add[c]
    for u in grp:
        _online_tile(sw_ref.at[u], pw_ref.at[u], mw_ref.at[u], lw_ref.at[u], None, accw_ref.at[u],
                     vw_ref[pl.ds(d0, wlen), kcol(u)], rows=rows, width=wlen, first=True)
        ob_ref[u, 2] = _finish(lw_ref.at[u], accw_ref.at[u])

    kxd = _positions_only(kx_ref[pl.ds(d0, tq), :])
    for u in grp:
        su = s_ref.at[u]
        kd_aug = jnp.concatenate([ks_ref[pl.ds(d0, tq), kcol(u)], kxd], axis=1)
        su[:, 0:tq] = lax.dot_general(q_aug0[u], kd_aug, _NT, preferred_element_type=F32)
        for c in range(nqc):
            cs = slice(c * LANES, (c + 1) * LANES)
            su[:, cs] = su[:, cs] + causal_add[c]
    for u in grp:
        _online_tile(*stats(u), vs_ref[pl.ds(d0, tq), kcol(u)], rows=rows, width=tq, first=True)

    def sel_body(kt, carry):
        @pl.when(used_ref[kt] != 0)
        def _():
            k0 = pl.multiple_of(kt * tk, tk)
            kxt = kx_ref[pl.ds(k0, tk), :]
            for u in grp:
                k_aug = jnp.concatenate([ks_ref[pl.ds(k0, tk), kcol(u)], kxt], axis=1)
                s_ref[u, :, 0:tk] = lax.dot_general(q_aug[u], k_aug, _NT, preferred_element_type=F32)
            for u in grp:
                _online_tile(*stats(u), vs_ref[pl.ds(k0, tk), kcol(u)], rows=rows, width=tk, first=False)
        return carry

    lax.fori_loop(0, (t0 + tk - 1) // tk, sel_body, 0)
    for u in grp:
        ob_ref[u, 1] = _finish(l_ref.at[u], acc_ref.at[u])

    gates = gt_ref[...].astype(F32)
    for u in grp:
        gts = pltpu.roll(gates, (LANES - (gp * gps + u) * hpg) % LANES, 1)
        gfull = jnp.dot(gts.astype(BF16), gx_ref[...], preferred_element_type=F32)
        for h in range(hpg):
            rs = slice(h * tq, (h + 1) * tq)
            cs = slice(h * HEAD_DIM, (h + 1) * HEAD_DIM)
            mix = jnp.zeros((tq, HEAD_DIM), F32)
            for br in range(3):
                z = z_refs[br][u][:, cs].astype(F32)
                gate = gfull[:, (br * hpg + h) * LANES:(br * hpg + h + 1) * LANES]
                mix = mix + gate * ob_ref[u, br, rs, :] * _silu(z)
            o_ref[:, u * gw + h * HEAD_DIM:u * gw + (h + 1) * HEAD_DIM] = mix.astype(o_ref.dtype)


def _nsa_attention(pieces, proj, gates, win, kvc, ovl_t, *, batch, seq):
    m = proj.shape[0]
    tq = min(128, seq)
    tk = min(512, seq)
    nt = seq // tq
    nb = seq // NSA_SEL_BLOCK
    nc = seq // NSA_CMP_STRIDE
    hpg, g_n, dh = NSA_HPG, NSA_KV_HEADS, HEAD_DIM
    gps = NSA_GROUPS_PER_STEP
    ngp = g_n // gps
    gw = hpg * dh
    q_cols = N_HEADS * dh
    kv_col0 = q_cols // (gps * dh)
    z_col0 = (q_cols + 3 * 2 * g_n * dh) // gw
    rows = hpg * tq
    wlen = NSA_WINDOW + tq
    sq = pl.Squeezed()
    assert nb <= POS_COL and g_n % gps == 0
    kx = _key_extras(jnp.arange(seq), NSA_SEL_BLOCK)
    wx = _key_extras(jnp.arange(seq + NSA_WINDOW))
    cx = _key_extras(jnp.arange(nc) * NSA_CMP_STRIDE + (NSA_CMP_LEN - 1))
    gsrc = (jnp.arange(3 * hpg) // hpg) * N_HEADS + jnp.arange(3 * hpg) % hpg
    gx = (jnp.arange(LANES)[:, None] == jnp.repeat(gsrc, LANES)[None, :]).astype(BF16)

    once = pl.Buffered(1)

    def kv_spec(branch, kv):
        c = kv_col0 + (branch * 2 + kv) * ngp
        return pl.BlockSpec((seq, gps * dh), lambda b, g, i: (b, c + g), pipeline_mode=once)

    def z_spec(branch, u):
        c = z_col0 + branch * g_n + u
        return pl.BlockSpec((tq, gw), lambda b, g, i: (b * nt + i, c + g * gps))

    win_spec = lambda kv: pl.BlockSpec((sq, seq + NSA_WINDOW, gps * dh), lambda b, g, i: (b, 0, kv * ngp + g),
                                       pipeline_mode=once)
    cmp_spec = lambda kv: pl.BlockSpec((sq, gps, nc, dh), lambda b, g, i: (kv, b * ngp + g, 0, 0))
    full = lambda arr: pl.BlockSpec(arr.shape, lambda b, g, i: (0,) * arr.ndim, pipeline_mode=once)
    in_specs = (
        [pl.BlockSpec(memory_space=pltpu.SMEM),
         pl.BlockSpec((tq, gps * gw), lambda b, g, i: (b * nt + i, g))]
        + [z_spec(br, u) for br in range(3) for u in range(gps)]
        + [pl.BlockSpec((tq, LANES), lambda b, g, i: (b * nt + i, 0)),
           kv_spec(1, 0), kv_spec(1, 1),
           win_spec(0), win_spec(1),
           cmp_spec(0), cmp_spec(1),
           full(kx), full(wx), full(cx), full(ovl_t), full(gx)])
    kern = functools.partial(_nsa_kernel, tq=tq, tk=tk, n_top=min(NSA_SEL_TOPN, nb), gps=gps)
    return pl.pallas_call(
        kern,
        grid=(batch, ngp, nt),
        in_specs=in_specs,
        out_specs=pl.BlockSpec((tq, gps * gw), lambda b, g, i: (b * nt + i, g)),
        out_shape=jax.ShapeDtypeStruct((m, q_cols), BF16),
        scratch_shapes=[pltpu.VMEM((gps, rows, tk), F32),
                        pltpu.VMEM((gps, rows, tk), BF16),
                        pltpu.VMEM((gps, rows, LANES), F32),
                        pltpu.VMEM((gps, rows, LANES), F32),
                        pltpu.VMEM((gps, rows, LANES), F32),
                        pltpu.VMEM((gps, rows, dh), F32),
                        pltpu.VMEM((gps, rows, wlen), F32),
                        pltpu.VMEM((gps, rows, wlen), BF16),
                        pltpu.VMEM((gps, rows, LANES), F32),
                        pltpu.VMEM((gps, rows, LANES), F32),
                        pltpu.VMEM((gps, rows, dh), F32),
                        pltpu.VMEM((gps, 3, rows, dh), F32),
                        pltpu.SMEM((seq // tk,), jnp.int32)],
        compiler_params=pltpu.CompilerParams(
            dimension_semantics=("arbitrary", "arbitrary", "arbitrary"), vmem_limit_bytes=VMEM_LIMIT),
        name="nsa_attention",
    )(pieces, proj, *([proj] * (3 * gps)), gates, proj, proj, win, win, kvc, kvc, kx, wx, cx, ovl_t, gx)


def _moba_kernel(sl_ref, q_ref, z_ref, k_ref, v_ref, mx_ref, o_ref,
                 km_ref, s_ref, p_ref, m_ref, l_ref, a_ref, acc_ref, *, tq, tk, n_top, gps):
    gp = pl.program_id(1)
    i = pl.program_id(2)
    hpg = MOBA_HPG
    gw = hpg * HEAD_DIM
    rows = hpg * tq
    seq = k_ref.shape[0]
    nbm = seq // MOBA_BLOCK
    bpt = tk // MOBA_BLOCK
    grp = range(gps)
    kcol = lambda u: slice(u * HEAD_DIM, (u + 1) * HEAD_DIM)
    stats = lambda u: (s_ref.at[u], p_ref.at[u], m_ref.at[u], l_ref.at[u], a_ref.at[u], acc_ref.at[u])

    @pl.when(i == 0)
    def _():
        blk = lax.broadcasted_iota(jnp.int32, (nbm, seq), 1) >> (MOBA_BLOCK.bit_length() - 1)
        avg = jnp.where(blk == lax.broadcasted_iota(jnp.int32, (nbm, seq), 0), 1.0 / MOBA_BLOCK, 0.0)
        for u in grp:
            km_ref[u] = jnp.dot(avg.astype(BF16), k_ref[:, kcol(u)], preferred_element_type=F32)

    n_io = lax.broadcasted_iota(jnp.int32, (nbm, rows), 0)
    past = n_io < i
    q_aug0, q_aug = [], []
    for u in grp:
        qa = _stack_heads(q_ref, hpg, u * gw)
        qx0 = _slope_extras(sl_ref, (gp * gps + u) * hpg, hpg, tq).astype(BF16)
        km = km_ref[u]
        km_hi = km.astype(BF16)
        km_lo = (km - km_hi.astype(F32)).astype(BF16)
        sb = (lax.dot_general(km_hi, qa, _NT, preferred_element_type=F32)
              + lax.dot_general(km_lo, qa, _NT, preferred_element_type=F32))
        sb = jnp.where(past, sb, NEG_INF)
        rank = jnp.zeros((nbm, rows), jnp.int32)
        for mm in range(nbm):
            row = sb[mm:mm + 1, :]
            beats = (row > sb) | ((row == sb) & (n_io > mm))
            rank = rank + beats.astype(jnp.int32)
        selneg = jnp.where(past & (rank < n_top), 0.0, NEG_INF).astype(F32)
        qx = qx0 + _pad_rows_to_lanes(selneg).T.astype(BF16)
        q_aug0.append(jnp.concatenate([qa, qx0], axis=1))
        q_aug.append(jnp.concatenate([qa, qx], axis=1))

    d0 = pl.multiple_of(i * tq, tq)
    mxd = _positions_only(mx_ref[pl.ds(d0, tq), :])
    causal_add = [_causal_add(hpg, tq, c, False) for c in range(tq // LANES)]
    for u in grp:
        su = s_ref.at[u]
        kd_aug = jnp.concatenate([k_ref[pl.ds(d0, tq), kcol(u)], mxd], axis=1)
        su[:, 0:tq] = lax.dot_general(q_aug0[u], kd_aug, _NT, preferred_element_type=F32)
        for c in range(tq // LANES):
            cs = slice(c * LANES, (c + 1) * LANES)
            su[:, cs] = su[:, cs] + causal_add[c]
    for u in grp:
        _online_tile(*stats(u), v_ref[pl.ds(d0, tq), kcol(u)], rows=rows, width=tq, first=True)

    def body(t, carry):
        k0 = pl.multiple_of(t * tk, tk)
        mxt = mx_ref[pl.ds(k0, tk), :]
        for u in grp:
            k_aug = jnp.concatenate([k_ref[pl.ds(k0, tk), kcol(u)], mxt], axis=1)
            s_ref[u] = lax.dot_general(q_aug[u], k_aug, _NT, preferred_element_type=F32)
        for u in grp:
            _online_tile(*stats(u), v_ref[pl.ds(k0, tk), kcol(u)], rows=rows, width=tk, first=False)
        return carry

    lax.fori_loop(0, (i + bpt - 1) // bpt, body, 0)
    for u in grp:
        o = _finish(l_ref.at[u], acc_ref.at[u])
        for h in range(hpg):
            cs = slice(u * gw + h * HEAD_DIM, u * gw + (h + 1) * HEAD_DIM)
            z = z_ref[:, cs].astype(F32)
            o_ref[:, cs] = (o[h * tq:(h + 1) * tq] * _silu(z)).astype(o_ref.dtype)


def _moba_attention(pieces, qz, kv, *, batch, seq):
    m = qz.shape[0]
    tq = MOBA_BLOCK
    tk = min(2 * MOBA_BLOCK, seq)
    nt = seq // tq
    hpg, g_n, dh = MOBA_HPG, MOBA_KV_HEADS, HEAD_DIM
    gps = MOBA_GROUPS_PER_STEP
    ngp = g_n // gps
    gw = gps * hpg * dh
    rows = hpg * tq
    nbm = seq // MOBA_BLOCK
    assert nbm <= POS_COL and g_n % gps == 0
    mx = _key_extras(jnp.arange(seq), MOBA_BLOCK)
    in_specs = [
        pl.BlockSpec(memory_space=pltpu.SMEM),
        pl.BlockSpec((tq, gw), lambda b, g, i: (b * nt + i, g)),
        pl.BlockSpec((tq, gw), lambda b, g, i: (b * nt + i, ngp + g)),
        pl.BlockSpec((seq, gps * dh), lambda b, g, i: (b, g)),
        pl.BlockSpec((seq, gps * dh), lambda b, g, i: (b, ngp + g)),
        pl.BlockSpec(mx.shape, lambda b, g, i: (0, 0)),
    ]
    kern = functools.partial(_moba_kernel, tq=tq, tk=tk, n_top=min(MOBA_TOPK, nbm), gps=gps)
    return pl.pallas_call(
        kern,
        grid=(batch, ngp, nt),
        in_specs=in_specs,
        out_specs=pl.BlockSpec((tq, gw), lambda b, g, i: (b * nt + i, g)),
        out_shape=jax.ShapeDtypeStruct((m, N_HEADS * dh), BF16),
        scratch_shapes=[pltpu.VMEM((gps, nbm, dh), F32),
                        pltpu.VMEM((gps, rows, tk), F32),
                        pltpu.VMEM((gps, rows, tk), BF16),
                        pltpu.VMEM((gps, rows, LANES), F32),
                        pltpu.VMEM((gps, rows, LANES), F32),
                        pltpu.VMEM((gps, rows, LANES), F32),
                        pltpu.VMEM((gps, rows, dh), F32)],
        compiler_params=pltpu.CompilerParams(
            dimension_semantics=("arbitrary", "arbitrary", "arbitrary"), vmem_limit_bytes=VMEM_LIMIT),
        name="moba_attention",
    )(pieces, qz, qz, kv, kv, mx)


def _nsa_layer(h, norm_g, w_in, pos_k, pos_v, w1_k, w2_k, w1_v, w2_v, w_out, *, batch, seq):
    dh, g_n = HEAD_DIM, NSA_KV_HEADS
    q_cols = N_HEADS * dh
    kv_cols = 3 * 2 * g_n * dh
    n_main = q_cols + kv_cols + 3 * q_cols
    n_gate = w_in.shape[1] - n_main
    assert n_gate == 3 * N_HEADS <= LANES and n_main % LANES == 0
    (xn,) = _rmsnorm(h, norm_g[None, :], BF16)
    scale = jnp.where(jnp.arange(n_main) < q_cols, dh ** -0.5 * LOG2E, 1.0).astype(F32)[None, :]
    w_t = w_in.T
    proj = _matmul(xn, w_t, out_dtype=BF16, scale=scale, n_cols=n_main, w_is_nk=True,
                   name="nsa_in_proj")
    w_gate_t = jnp.pad(w_t[n_main:], ((0, LANES - n_gate), (0, 0)))
    gates = _matmul(xn, w_gate_t, out_dtype=BF16, scale=jnp.ones((1, LANES), F32), sigmoid=True,
                    w_is_nk=True, name="nsa_gate_proj")

    half = NSA_CMP_STRIDE
    nr = seq // half
    pos = jnp.stack([pos_k, pos_v])
    pa = pos[:, :half].reshape(2, 1, half * dh)
    pb = pos[:, half:].reshape(2, 1, half * dh)
    w1 = jnp.stack([w1_k, w1_v])
    w1a = w1[:, :half].reshape(2, half * dh, dh).astype(BF16)
    w1b = w1[:, half:].reshape(2, half * dh, dh).astype(BF16)
    w2 = jnp.stack([w2_k, w2_v]).astype(BF16)
    kvc = _compress(proj, q_cols, pa, pb, w1a, w1b, w2, batch=batch, seq=seq)

    wcol = q_cols + 2 * 2 * g_n * dh
    win = proj[:, wcol:wcol + 2 * g_n * dh].reshape(batch, seq, 2 * g_n * dh)
    win = jnp.pad(win, ((0, 0), (NSA_WINDOW, 0), (0, 0)))

    nb = seq // NSA_SEL_BLOCK
    cstart = jnp.arange(nr)[None, :] * NSA_CMP_STRIDE
    sstart = jnp.arange(nb)[:, None] * NSA_SEL_BLOCK
    ovl_t = ((cstart <= sstart + NSA_SEL_BLOCK - 1)
             & (cstart + NSA_CMP_LEN - 1 >= sstart)
             & (jnp.arange(nr)[None, :] < (seq - NSA_CMP_LEN) // NSA_CMP_STRIDE + 1)).astype(BF16)
    mix = _nsa_attention(_slope_pieces(), proj, gates, win, kvc, ovl_t, batch=batch, seq=seq)
    return _matmul(mix, w_out, out_dtype=F32, residual=h, name="nsa_out_proj")


def _moba_shared_kv(xn_kv, kv_w):
    one = jnp.ones((1, kv_w.shape[1]), F32)
    return _matmul(xn_kv, kv_w, out_dtype=BF16, scale=one, name="moba_kv_proj")


def _moba_layer(h, xn_q, kv, w_in, w_out, *, batch, seq):
    dh = HEAD_DIM
    q_cols = N_HEADS * dh
    scale = jnp.where(jnp.arange(w_in.shape[1]) < q_cols, dh ** -0.5 * LOG2E, 1.0).astype(F32)[None, :]
    qz = _matmul(xn_q, w_in, out_dtype=BF16, scale=scale, name="moba_in_proj")
    o = _moba_attention(_slope_pieces(), qz, kv, batch=batch, seq=seq)
    return _matmul(o, w_out, out_dtype=F32, residual=h, name="moba_out_proj")


def kernel(x, a_norm_g, a_w_in, a_cmp_pos_k, a_cmp_pos_v, a_cmp_w1_k, a_cmp_w2_k, a_cmp_w1_v, a_cmp_w2_v,
           a_w_out, kv_norm_g, kv_w, b_norm_g, b_w_in, b_w_out, final_norm_g):
    batch, seq, d = x.shape
    h = x.reshape(batch * seq, d)
    for layer in range(a_norm_g.shape[0]):
        h = _nsa_layer(h, a_norm_g[layer], a_w_in[layer], a_cmp_pos_k[layer], a_cmp_pos_v[layer],
                       a_cmp_w1_k[layer], a_cmp_w2_k[layer], a_cmp_w1_v[layer], a_cmp_w2_v[layer],
                       a_w_out[layer], batch=batch, seq=seq)
    kv = None
    for layer in range(b_norm_g.shape[0]):
        if layer == 0:
            xn_kv, xn_q = _rmsnorm(h, jnp.stack([kv_norm_g, b_norm_g[layer]]), BF16)
            kv = _moba_shared_kv(xn_kv, kv_w)
        else:
            (xn_q,) = _rmsnorm(h, b_norm_g[layer][None, :], BF16)
        h = _moba_layer(h, xn_q, kv, b_w_in[layer], b_w_out[layer], batch=batch, seq=seq)
    (out,) = _rmsnorm(h, final_norm_g[None, :], F32)
    return out.reshape(batch, seq, d)
```

```python
import functools

import jax
import jax.numpy as jnp
from jax import lax
from jax.experimental import pallas as pl
from jax.experimental.pallas import tpu as pltpu

F32 = jnp.float32
BF16 = jnp.bfloat16

N_HEADS = 32
HEAD_DIM = 128
NSA_KV_HEADS = 4
NSA_HPG = N_HEADS // NSA_KV_HEADS
NSA_CMP_LEN = 32
NSA_CMP_STRIDE = 16
NSA_SEL_BLOCK = 64
NSA_SEL_TOPN = 16
NSA_WINDOW = 512
MOBA_KV_HEADS = 8
MOBA_HPG = N_HEADS // MOBA_KV_HEADS
MOBA_BLOCK = 256
MOBA_TOPK = 3
RMS_EPS = 1e-6
NEG_INF = -1e30
FORCE_SCORE = 1e9
TOPK_REMOVED = -3.0e38
LOG2E = 1.4426950408889634

LANES = 128
VMEM_LIMIT = 56 * 1024 * 1024
NSA_GROUPS_PER_STEP = 2
MOBA_GROUPS_PER_STEP = 4
POS_COL = 120
POS_SPLIT = 64

_NT = (((1,), (1,)), ((), ()))


def _sigmoid(x):
    return 1.0 / (1.0 + jnp.exp(-x))


def _silu(x):
    h = 0.5 * x
    return h + h * jnp.tanh(h)


def _rmsnorm_kernel(x_ref, g_ref, *o_refs):
    x = x_ref[...].astype(F32)
    y = x * lax.rsqrt(jnp.mean(x * x, axis=-1, keepdims=True) + RMS_EPS)
    for i, o_ref in enumerate(o_refs):
        o_ref[...] = (y * g_ref[i:i + 1, :]).astype(o_ref.dtype)


def _rmsnorm(x, gains, out_dtype):
    m, d = x.shape
    n = gains.shape[0]
    tm = min(512, m)
    outs = pl.pallas_call(
        _rmsnorm_kernel,
        grid=(m // tm,),
        in_specs=[pl.BlockSpec((tm, d), lambda i: (i, 0)),
                  pl.BlockSpec((n, d), lambda i: (0, 0))],
        out_specs=[pl.BlockSpec((tm, d), lambda i: (i, 0)) for _ in range(n)],
        out_shape=[jax.ShapeDtypeStruct((m, d), out_dtype) for _ in range(n)],
        compiler_params=pltpu.CompilerParams(
            dimension_semantics=("arbitrary",), vmem_limit_bytes=VMEM_LIMIT),
        name="rmsnorm",
    )(x, gains)
    return outs


def _mm_kernel(a_ref, w_ref, x_ref, o_ref, *, w_is_nk, epilogue):
    w = w_ref[...].astype(BF16)
    if w_is_nk:
        acc = lax.dot_general(a_ref[...], w, _NT, preferred_element_type=F32)
    else:
        acc = jnp.dot(a_ref[...], w, preferred_element_type=F32)
    if epilogue == "residual":
        out = x_ref[...] + acc
    elif epilogue == "sigmoid":
        out = _sigmoid(acc)
    else:
        out = acc * x_ref[...]
    o_ref[...] = out.astype(o_ref.dtype)


def _matmul(a, w, *, out_dtype, scale=None, residual=None, sigmoid=False, n_cols=None, w_is_nk=False, name):
    m, k = a.shape
    n = (w.shape[0] if w_is_nk else w.shape[1]) if n_cols is None else n_cols
    tn = min(512, n)
    assert n % tn == 0
    if n // tn >= 32:
        tm = min(2048, m)
        a_spec = pl.BlockSpec((tm, k), lambda i, j: (i, 0), pipeline_mode=pl.Buffered(1))
    else:
        tm = min(1024, m)
        a_spec = pl.BlockSpec((tm, k), lambda i, j: (i, 0))
    w_spec = pl.BlockSpec((tn, k), lambda i, j: (j, 0)) if w_is_nk else pl.BlockSpec((k, tn), lambda i, j: (0, j))
    o_spec = pl.BlockSpec((tm, tn), lambda i, j: (i, j))
    epilogue = "residual" if residual is not None else ("sigmoid" if sigmoid else "scale")
    kern = functools.partial(_mm_kernel, w_is_nk=w_is_nk, epilogue=epilogue)
    if residual is None:
        extra, extra_spec = scale, pl.BlockSpec((1, tn), lambda i, j: (0, j))
    else:
        extra, extra_spec = residual, o_spec
    return pl.pallas_call(
        kern,
        grid=(m // tm, n // tn),
        in_specs=[a_spec, w_spec, extra_spec],
        out_specs=o_spec,
        out_shape=jax.ShapeDtypeStruct((m, n), out_dtype),
        compiler_params=pltpu.CompilerParams(
            dimension_semantics=("arbitrary", "arbitrary"), vmem_limit_bytes=VMEM_LIMIT),
        name=name,
    )(a, w, extra)


def _cmp_kernel(raw_ref, pa_ref, pb_ref, w1a_ref, w1b_ref, w2_ref, o_ref, raw32_ref):
    half = NSA_CMP_STRIDE
    nr = raw_ref.shape[0] // half
    raw32_ref[...] = raw_ref[...].astype(F32)
    r = jnp.concatenate([raw32_ref[pl.ds(l, nr, stride=half), :] for l in range(half)], axis=1)
    xa = (r + pa_ref[...]).astype(BF16)
    xb = (r + pb_ref[...]).astype(BF16)
    ya = jnp.dot(xa, w1a_ref[...], preferred_element_type=F32)
    yb = jnp.dot(xb, w1b_ref[...], preferred_element_type=F32)
    hid = ya + pltpu.roll(yb, nr - 1, 0)
    hid = hid * _sigmoid(hid)
    o_ref[...] = jnp.dot(hid.astype(BF16), w2_ref[...], preferred_element_type=F32).astype(o_ref.dtype)


def _compress(proj, col0, pa, pb, w1a, w1b, w2, *, batch, seq):
    g_n, dh = NSA_KV_HEADS, HEAD_DIM
    nr = seq // NSA_CMP_STRIDE
    kk = NSA_CMP_STRIDE * dh
    cb = col0 // dh
    sq = pl.Squeezed()
    wspec = lambda shape: pl.BlockSpec((sq,) + shape, lambda t, i: (t, 0, 0))
    return pl.pallas_call(
        _cmp_kernel,
        grid=(2, batch * g_n),
        in_specs=[pl.BlockSpec((seq, dh), lambda t, i: (i // g_n, cb + t * g_n + i % g_n)),
                  wspec((1, kk)), wspec((1, kk)), wspec((kk, dh)), wspec((kk, dh)), wspec((dh, dh))],
        out_specs=pl.BlockSpec((sq, sq, nr, dh), lambda t, i: (t, i, 0, 0)),
        out_shape=jax.ShapeDtypeStruct((2, batch * g_n, nr, dh), BF16),
        scratch_shapes=[pltpu.VMEM((seq, dh), F32)],
        compiler_params=pltpu.CompilerParams(
            dimension_semantics=("arbitrary", "arbitrary"), vmem_limit_bytes=VMEM_LIMIT),
        name="nsa_compress",
    )(proj, pa, pb, w1a, w1b, w2)


def _stack_heads(ref, n_heads, col0=0):
    return jnp.concatenate(
        [ref[:, col0 + h * HEAD_DIM:col0 + (h + 1) * HEAD_DIM] for h in range(n_heads)], axis=0)


def _query_in_tile(n_heads, tq, width):
    assert tq & (tq - 1) == 0, "query tile must be a power of two"
    return lax.broadcasted_iota(jnp.int32, (n_heads * tq, width), 0) & (tq - 1)


def _slope_extras(sl_ref, head0, n_heads, tq):
    lane = lax.broadcasted_iota(jnp.int32, (tq, LANES), 1)
    blocks = []
    for h in range(n_heads):
        x = jnp.zeros((tq, LANES), F32)
        for piece in range(3):
            hit = (lane == POS_COL + piece) | (lane == POS_COL + 3 + piece)
            x = jnp.where(hit, sl_ref[piece, head0 + h], x)
        blocks.append(x)
    return jnp.concatenate(blocks, axis=0)


def _causal_add(n_heads, tq, chunk, strict_future):
    qq = _query_in_tile(n_heads, tq, LANES)
    kk = lax.broadcasted_iota(jnp.int32, (n_heads * tq, LANES), 1) + chunk * LANES
    keep = (kk > qq) if strict_future else (kk <= qq)
    return jnp.where(keep, 0.0, NEG_INF)


def _positions_only(kx_tile):
    lane = lax.broadcasted_iota(jnp.int32, kx_tile.shape, 1)
    return jnp.where(lane >= POS_COL, kx_tile, jnp.zeros_like(kx_tile))


def _topk_rows(score, k):
    n = score.shape[0]
    row = lax.broadcasted_iota(jnp.int32, score.shape, 0)
    picked = jnp.zeros(score.shape, F32)
    rest = score
    for _ in range(k):
        best = jnp.max(rest, axis=0, keepdims=True)
        first = jnp.min(jnp.where(rest == best, row, n), axis=0, keepdims=True)
        hit = row == first
        picked = jnp.where(hit, 1.0, picked)
        rest = jnp.where(hit, TOPK_REMOVED, rest)
    return picked


def _pad_rows_to_lanes(x):
    n = x.shape[0]
    assert n <= POS_COL, "block one-hot columns must stay clear of the slope/position columns"
    return jnp.concatenate([x, jnp.zeros((LANES - n, x.shape[1]), x.dtype)], axis=0)


def _online_tile(s_ref, p_ref, m_ref, l_ref, a_ref, acc_ref, v, *, rows, width, first):
    nch = width // LANES
    mt = s_ref[:, 0:LANES]
    for c in range(1, nch):
        mt = jnp.maximum(mt, s_ref[:, c * LANES:(c + 1) * LANES])
    mrow = jnp.max(mt, axis=-1, keepdims=True)
    if first:
        m_new = jnp.broadcast_to(mrow, (rows, LANES))
    else:
        m_old = m_ref[...]
        m_new = jnp.maximum(m_old, mrow)
        a_ref[...] = jnp.exp2(m_old - m_new)
    m_ref[...] = m_new
    ps = None
    for c in range(nch):
        cs = slice(c * LANES, (c + 1) * LANES)
        p = jnp.exp2(s_ref[:, cs] - m_new)
        p_ref[:, cs] = p.astype(BF16)
        ps = p if ps is None else ps + p
    if first:
        l_ref[...] = ps
    else:
        l_ref[...] = a_ref[...] * l_ref[...] + ps
    pv = jnp.dot(p_ref[:, 0:width], v, preferred_element_type=F32)
    if first:
        acc_ref[...] = pv
    else:
        acc_ref[...] = a_ref[...] * acc_ref[...] + pv


def _finish(l_ref, acc_ref):
    return acc_ref[...] * (1.0 / jnp.sum(l_ref[...], axis=-1, keepdims=True))


def _key_extras(pos, block=None):
    pos = pos[:, None]
    col = jnp.arange(LANES)[None, :]
    x = jnp.zeros((pos.shape[0], LANES), F32)
    if block is not None:
        x = jnp.where(col == pos // block, 1.0, x)
    x = jnp.where((col >= POS_COL) & (col < POS_COL + 3), (pos // POS_SPLIT) * POS_SPLIT, x)
    x = jnp.where((col >= POS_COL + 3) & (col < POS_COL + 6), pos % POS_SPLIT, x)
    return x.astype(BF16)


def _slope_pieces():
    s = 2.0 ** (-8.0 * jnp.arange(1, N_HEADS + 1, dtype=F32) / N_HEADS) * LOG2E
    a = s.astype(BF16).astype(F32)
    b = (s - a).astype(BF16).astype(F32)
    c = (s - a - b).astype(BF16).astype(F32)
    return jnp.stack([a, b, c])


def _nsa_kernel(sl_ref, q_ref, *refs, tq, tk, n_top, gps):
    z_refs = [refs[br * gps:(br + 1) * gps] for br in range(3)]
    (gt_ref, ks_ref, vs_ref, kw_ref, vw_ref, kc_ref, vc_ref, kx_ref, wx_ref, cx_ref, ovl_ref, gx_ref, o_ref,
     s_ref, p_ref, m_ref, l_ref, a_ref, acc_ref, sw_ref, pw_ref, mw_ref, lw_ref, accw_ref,
     ob_ref, used_ref) = refs[3 * gps:]
    gp = pl.program_id(1)
    i = pl.program_id(2)
    t0 = i * tq
    hpg = NSA_HPG
    gw = hpg * HEAD_DIM
    rows = hpg * tq
    grp = range(gps)
    kcol = lambda u: slice(u * HEAD_DIM, (u + 1) * HEAD_DIM)
    stats = lambda u: (s_ref.at[u], p_ref.at[u], m_ref.at[u], l_ref.at[u], a_ref.at[u], acc_ref.at[u])
    nqc = tq // LANES
    d0 = pl.multiple_of(t0, tq)
    qa = [_stack_heads(q_ref, hpg, u * gw) for u in grp]
    qx0 = [_slope_extras(sl_ref, (gp * gps + u) * hpg, hpg, tq).astype(BF16) for u in grp]
    q_aug0 = [jnp.concatenate([qa[u], qx0[u]], axis=1) for u in grp]

    nc = kc_ref.shape[1]
    cx = cx_ref[...]
    cend = lax.broadcasted_iota(jnp.int32, (1, nc), 1) * NSA_CMP_STRIDE + (NSA_CMP_LEN - 1)
    tqv = t0 + (lax.broadcasted_iota(jnp.int32, (rows, 1), 0) & (tq - 1))
    mask = cend <= tqv
    row_sees_block = tqv >= NSA_CMP_LEN - 1
    ovl = ovl_ref[...]
    nb = ovl.shape[0]
    j = lax.broadcasted_iota(jnp.int32, (nb, tq), 0)
    blkq = (t0 + lax.broadcasted_iota(jnp.int32, (nb, tq), 1)) >> (NSA_SEL_BLOCK.bit_length() - 1)
    forced = (j == 0) | (j == blkq) | (j == blkq - 1)
    blk0 = t0 >> (NSA_SEL_BLOCK.bit_length() - 1)
    q_aug = []
    any_sel = None
    for u in grp:
        kc_aug = jnp.concatenate([kc_ref[u], cx], axis=1)
        s = lax.dot_general(q_aug0[u], kc_aug, _NT, preferred_element_type=F32)
        s = jnp.where(mask, s, NEG_INF)
        mx = jnp.max(s, axis=-1, keepdims=True)
        e = jnp.exp2(s - mx)
        den = jnp.sum(e, axis=-1, keepdims=True)
        p = e * jnp.where(row_sees_block, 1.0 / den, 0.0)
        ob_ref[u, 0] = jnp.dot(p.astype(BF16), vc_ref[u], preferred_element_type=F32)
        psum = p[0:tq]
        for h in range(1, hpg):
            psum = psum + p[h * tq:(h + 1) * tq]
        p_hi = psum.astype(BF16)
        p_lo = (psum - p_hi.astype(F32)).astype(BF16)
        imp = (lax.dot_general(ovl, p_hi, _NT, preferred_element_type=F32)
               + lax.dot_general(ovl, p_lo, _NT, preferred_element_type=F32))
        imp = jnp.where(forced, FORCE_SCORE, jnp.where(j > blkq, NEG_INF, imp))
        selneg = jnp.where((_topk_rows(imp, n_top) > 0.5) & (j < blk0), 0.0, NEG_INF).astype(F32)
        any_sel = selneg if any_sel is None else jnp.maximum(any_sel, selneg)
        selneg_q = _pad_rows_to_lanes(selneg).T.astype(BF16)
        qx = qx0[u] + jnp.concatenate([selneg_q] * hpg, axis=0)
        q_aug.append(jnp.concatenate([qa[u], qx], axis=1))
    bpt = tk // NSA_SEL_BLOCK
    for t in range(nb // bpt):
        used_ref[t] = (jnp.max(any_sel[t * bpt:(t + 1) * bpt, :]) > 0.5 * NEG_INF).astype(jnp.int32)

    causal_add = [_causal_add(hpg, tq, c, False) for c in range(nqc)]
    future_add = [_causal_add(hpg, tq, c, True) for c in range(nqc)]
    wlen = NSA_WINDOW + tq
    wxt = wx_ref[pl.ds(d0, wlen), :]
    for u in grp:
        su = sw_ref.at[u]
        kw_aug = jnp.concatenate([kw_ref[pl.ds(d0, wlen), kcol(u)], wxt], axis=1)
        su[...] = lax.dot_general(q_aug0[u], kw_aug, _NT, preferred_element_type=F32)
        for c in range(nqc):
            lo = slice(c * LANES, (c + 1) * LANES)
            su[:, lo] = su[:, lo] + future_add[c]
            hi = slice(NSA_WINDOW + c * LANES, NSA_WINDOW + (c + 1) * LANES)
            su[:, hi] = su[:, hi] + causal_add[c]
    for u in grp:
        _online_tile(sw_ref.at[u], pw_ref.at[u], mw_ref.at[u], lw_ref.at[u], None, accw_ref.at[u],
                     vw_ref[pl.ds(d0, wlen), kcol(u)], rows=rows, width=wlen, first=True)
        ob_ref[u, 2] = _finish(lw_ref.at[u], accw_ref.at[u])

    kxd = _positions_only(kx_ref[pl.ds(d0, tq), :])
    for u in grp:
        su = s_ref.at[u]
        kd_aug = jnp.concatenate([ks_ref[pl.ds(d0, tq), kcol(u)], kxd], axis=1)
        su[:, 0:tq] = lax.dot_general(q_aug0[u], kd_aug, _NT, preferred_element_type=F32)
        for c in range(nqc):
            cs = slice(c * LANES, (c + 1) * LANES)
            su[:, cs] = su[:, cs] + causal_add[c]
    for u in grp:
        _online_tile(*stats(u), vs_ref[pl.ds(d0, tq), kcol(u)], rows=rows, width=tq, first=True)

    def sel_body(kt, carry):
        @pl.when(used_ref[kt] != 0)
        def _():
            k0 = pl.multiple_of(kt * tk, tk)
            kxt = kx_ref[pl.ds(k0, tk), :]
            for u in grp:
                k_aug = jnp.concatenate([ks_ref[pl.ds(k0, tk), kcol(u)], kxt], axis=1)
                s_ref[u, :, 0:tk] = lax.dot_general(q_aug[u], k_aug, _NT, preferred_element_type=F32)
            for u in grp:
                _online_tile(*stats(u), vs_ref[pl.ds(k0, tk), kcol(u)], rows=rows, width=tk, first=False)
        return carry

    lax.fori_loop(0, (t0 + tk - 1) // tk, sel_body, 0)
    for u in grp:
        ob_ref[u, 1] = _finish(l_ref.at[u], acc_ref.at[u])

    gates = gt_ref[...].astype(F32)
    for u in grp:
        gts = pltpu.roll(gates, (LANES - (gp * gps + u) * hpg) % LANES, 1)
        gfull = jnp.dot(gts.astype(BF16), gx_ref[...], preferred_element_type=F32)
        for h in range(hpg):
            rs = slice(h * tq, (h + 1) * tq)
            cs = slice(h * HEAD_DIM, (h + 1) * HEAD_DIM)
            mix = jnp.zeros((tq, HEAD_DIM), F32)
            for br in range(3):
                z = z_refs[br][u][:, cs].astype(F32)
                gate = gfull[:, (br * hpg + h) * LANES:(br * hpg + h + 1) * LANES]
                mix = mix + gate * ob_ref[u, br, rs, :] * _silu(z)
            o_ref[:, u * gw + h * HEAD_DIM:u * gw + (h + 1) * HEAD_DIM] = mix.astype(o_ref.dtype)


def _nsa_attention(pieces, proj, gates, win, kvc, ovl_t, *, batch, seq):
    m = proj.shape[0]
    tq = min(128, seq)
    tk = min(512, seq)
    nt = seq // tq
    nb = seq // NSA_SEL_BLOCK
    nc = seq // NSA_CMP_STRIDE
    hpg, g_n, dh = NSA_HPG, NSA_KV_HEADS, HEAD_DIM
    gps = NSA_GROUPS_PER_STEP
    ngp = g_n // gps
    gw = hpg * dh
    q_cols = N_HEADS * dh
    kv_col0 = q_cols // (gps * dh)
    z_col0 = (q_cols + 3 * 2 * g_n * dh) // gw
    rows = hpg * tq
    wlen = NSA_WINDOW + tq
    sq = pl.Squeezed()
    assert nb <= POS_COL and g_n % gps == 0
    kx = _key_extras(jnp.arange(seq), NSA_SEL_BLOCK)
    wx = _key_extras(jnp.arange(seq + NSA_WINDOW))
    cx = _key_extras(jnp.arange(nc) * NSA_CMP_STRIDE + (NSA_CMP_LEN - 1))
    gsrc = (jnp.arange(3 * hpg) // hpg) * N_HEADS + jnp.arange(3 * hpg) % hpg
    gx = (jnp.arange(LANES)[:, None] == jnp.repeat(gsrc, LANES)[None, :]).astype(BF16)

    once = pl.Buffered(1)

    def kv_spec(branch, kv):
        c = kv_col0 + (branch * 2 + kv) * ngp
        return pl.BlockSpec((seq, gps * dh), lambda b, g, i: (b, c + g), pipeline_mode=once)

    def z_spec(branch, u):
        c = z_col0 + branch * g_n + u
        return pl.BlockSpec((tq, gw), lambda b, g, i: (b * nt + i, c + g * gps))

    win_spec = lambda kv: pl.BlockSpec((sq, seq + NSA_WINDOW, gps * dh), lambda b, g, i: (b, 0, kv * ngp + g),
                                       pipeline_mode=once)
    cmp_spec = lambda kv: pl.BlockSpec((sq, gps, nc, dh), lambda b, g, i: (kv, b * ngp + g, 0, 0))
    full = lambda arr: pl.BlockSpec(arr.shape, lambda b, g, i: (0,) * arr.ndim, pipeline_mode=once)
    in_specs = (
        [pl.BlockSpec(memory_space=pltpu.SMEM),
         pl.BlockSpec((tq, gps * gw), lambda b, g, i: (b * nt + i, g))]
        + [z_spec(br, u) for br in range(3) for u in range(gps)]
        + [pl.BlockSpec((tq, LANES), lambda b, g, i: (b * nt + i, 0)),
           kv_spec(1, 0), kv_spec(1, 1),
           win_spec(0), win_spec(1),
           cmp_spec(0), cmp_spec(1),
           full(kx), full(wx), full(cx), full(ovl_t), full(gx)])
    kern = functools.partial(_nsa_kernel, tq=tq, tk=tk, n_top=min(NSA_SEL_TOPN, nb), gps=gps)
    return pl.pallas_call(
        kern,
        grid=(batch, ngp, nt),
        in_specs=in_specs,
        out_specs=pl.BlockSpec((tq, gps * gw), lambda b, g, i: (b * nt + i, g)),
        out_shape=jax.ShapeDtypeStruct((m, q_cols), BF16),
        scratch_shapes=[pltpu.VMEM((gps, rows, tk), F32),
                        pltpu.VMEM((gps, rows, tk), BF16),
                        pltpu.VMEM((gps, rows, LANES), F32),
                        pltpu.VMEM((gps, rows, LANES), F32),
                        pltpu.VMEM((gps, rows, LANES), F32),
                        pltpu.VMEM((gps, rows, dh), F32),
                        pltpu.VMEM((gps, rows, wlen), F32),
                        pltpu.VMEM((gps, rows, wlen), BF16),
                        pltpu.VMEM((gps, rows, LANES), F32),
                        pltpu.VMEM((gps, rows, LANES), F32),
                        pltpu.VMEM((gps, rows, dh), F32),
                        pltpu.VMEM((gps, 3, rows, dh), F32),
                        pltpu.SMEM((seq // tk,), jnp.int32)],
        compiler_params=pltpu.CompilerParams(
            dimension_semantics=("arbitrary", "arbitrary", "arbitrary"), vmem_limit_bytes=VMEM_LIMIT),
        name="nsa_attention",
    )(pieces, proj, *([proj] * (3 * gps)), gates, proj, proj, win, win, kvc, kvc, kx, wx, cx, ovl_t, gx)


def _moba_kernel(sl_ref, q_ref, z_ref, k_ref, v_ref, mx_ref, o_ref,
                 km_ref, s_ref, p_ref, m_ref, l_ref, a_ref, acc_ref, *, tq, tk, n_top, gps):
    gp = pl.program_id(1)
    i = pl.program_id(2)
    hpg = MOBA_HPG
    gw = hpg * HEAD_DIM
    rows = hpg * tq
    seq = k_ref.shape[0]
    nbm = seq // MOBA_BLOCK
    bpt = tk // MOBA_BLOCK
    grp = range(gps)
    kcol = lambda u: slice(u * HEAD_DIM, (u + 1) * HEAD_DIM)
    stats = lambda u: (s_ref.at[u], p_ref.at[u], m_ref.at[u], l_ref.at[u], a_ref.at[u], acc_ref.at[u])

    @pl.when(i == 0)
    def _():
        blk = lax.broadcasted_iota(jnp.int32, (nbm, seq), 1) >> (MOBA_BLOCK.bit_length() - 1)
        avg = jnp.where(blk == lax.broadcasted_iota(jnp.int32, (nbm, seq), 0), 1.0 / MOBA_BLOCK, 0.0)
        for u in grp:
            km_ref[u] = jnp.dot(avg.astype(BF16), k_ref[:, kcol(u)], preferred_element_type=F32)

    n_io = lax.broadcasted_iota(jnp.int32, (nbm, rows), 0)
    past = n_io < i
    q_aug0, q_aug = [], []
    for u in grp:
        qa = _stack_heads(q_ref, hpg, u * gw)
        qx0 = _slope_extras(sl_ref, (gp * gps + u) * hpg, hpg, tq).astype(BF16)
        km = km_ref[u]
        km_hi = km.astype(BF16)
        km_lo = (km - km_hi.astype(F32)).astype(BF16)
        sb = (lax.dot_general(km_hi, qa, _NT, preferred_element_type=F32)
              + lax.dot_general(km_lo, qa, _NT, preferred_element_type=F32))
        sb = jnp.where(past, sb, NEG_INF)
        rank = jnp.zeros((nbm, rows), jnp.int32)
        for mm in range(nbm):
            row = sb[mm:mm + 1, :]
            beats = (row > sb) | ((row == sb) & (n_io > mm))
            rank = rank + beats.astype(jnp.int32)
        selneg = jnp.where(past & (rank < n_top), 0.0, NEG_INF).astype(F32)
        qx = qx0 + _pad_rows_to_lanes(selneg).T.astype(BF16)
        q_aug0.append(jnp.concatenate([qa, qx0], axis=1))
        q_aug.append(jnp.concatenate([qa, qx], axis=1))

    d0 = pl.multiple_of(i * tq, tq)
    mxd = _positions_only(mx_ref[pl.ds(d0, tq), :])
    causal_add = [_causal_add(hpg, tq, c, False) for c in range(tq // LANES)]
    for u in grp:
        su = s_ref.at[u]
        kd_aug = jnp.concatenate([k_ref[pl.ds(d0, tq), kcol(u)], mxd], axis=1)
        su[:, 0:tq] = lax.dot_general(q_aug0[u], kd_aug, _NT, preferred_element_type=F32)
        for c in range(tq // LANES):
            cs = slice(c * LANES, (c + 1) * LANES)
            su[:, cs] = su[:, cs] + causal_add[c]
    for u in grp:
        _online_tile(*stats(u), v_ref[pl.ds(d0, tq), kcol(u)], rows=rows, width=tq, first=True)

    def body(t, carry):
        k0 = pl.multiple_of(t * tk, tk)
        mxt = mx_ref[pl.ds(k0, tk), :]
        for u in grp:
            k_aug = jnp.concatenate([k_ref[pl.ds(k0, tk), kcol(u)], mxt], axis=1)
            s_ref[u] = lax.dot_general(q_aug[u], k_aug, _NT, preferred_element_type=F32)
        for u in grp:
            _online_tile(*stats(u), v_ref[pl.ds(k0, tk), kcol(u)], rows=rows, width=tk, first=False)
        return carry

    lax.fori_loop(0, (i + bpt - 1) // bpt, body, 0)
    for u in grp:
        o = _finish(l_ref.at[u], acc_ref.at[u])
        for h in range(hpg):
            cs = slice(u * gw + h * HEAD_DIM, u * gw + (h + 1) * HEAD_DIM)
            z = z_ref[:, cs].astype(F32)
            o_ref[:, cs] = (o[h * tq:(h + 1) * tq] * _silu(z)).astype(o_ref.dtype)


def _moba_attention(pieces, qz, kv, *, batch, seq):
    m = qz.shape[0]
    tq = MOBA_BLOCK
    tk = min(2 * MOBA_BLOCK, seq)
    nt = seq // tq
    hpg, g_n, dh = MOBA_HPG, MOBA_KV_HEADS, HEAD_DIM
    gps = MOBA_GROUPS_PER_STEP
    ngp = g_n // gps
    gw = gps * hpg * dh
    rows = hpg * tq
    nbm = seq // MOBA_BLOCK
    assert nbm <= POS_COL and g_n % gps == 0
    mx = _key_extras(jnp.arange(seq), MOBA_BLOCK)
    in_specs = [
        pl.BlockSpec(memory_space=pltpu.SMEM),
        pl.BlockSpec((tq, gw), lambda b, g, i: (b * nt + i, g)),
        pl.BlockSpec((tq, gw), lambda b, g, i: (b * nt + i, ngp + g)),
        pl.BlockSpec((seq, gps * dh), lambda b, g, i: (b, g)),
        pl.BlockSpec((seq, gps * dh), lambda b, g, i: (b, ngp + g)),
        pl.BlockSpec(mx.shape, lambda b, g, i: (0, 0)),
    ]
    kern = functools.partial(_moba_kernel, tq=tq, tk=tk, n_top=min(MOBA_TOPK, nbm), gps=gps)
    return pl.pallas_call(
        kern,
        grid=(batch, ngp, nt),
        in_specs=in_specs,
        out_specs=pl.BlockSpec((tq, gw), lambda b, g, i: (b * nt + i, g)),
        out_shape=jax.ShapeDtypeStruct((m, N_HEADS * dh), BF16),
        scratch_shapes=[pltpu.VMEM((gps, nbm, dh), F32),
                        pltpu.VMEM((gps, rows, tk), F32),
                        pltpu.VMEM((gps, rows, tk), BF16),
                        pltpu.VMEM((gps, rows, LANES), F32),
                        pltpu.VMEM((gps, rows, LANES), F32),
                        pltpu.VMEM((gps, rows, LANES), F32),
                        pltpu.VMEM((gps, rows, dh), F32)],
        compiler_params=pltpu.CompilerParams(
            dimension_semantics=("arbitrary", "arbitrary", "arbitrary"), vmem_limit_bytes=VMEM_LIMIT),
        name="moba_attention",
    )(pieces, qz, qz, kv, kv, mx)


def _nsa_layer(h, norm_g, w_in, pos_k, pos_v, w1_k, w2_k, w1_v, w2_v, w_out, *, batch, seq):
    dh, g_n = HEAD_DIM, NSA_KV_HEADS
    q_cols = N_HEADS * dh
    kv_cols = 3 * 2 * g_n * dh
    n_main = q_cols + kv_cols + 3 * q_cols
    n_gate = w_in.shape[1] - n_main
    assert n_gate == 3 * N_HEADS <= LANES and n_main % LANES == 0
    (xn,) = _rmsnorm(h, norm_g[None, :], BF16)
    scale = jnp.where(jnp.arange(n_main) < q_cols, dh ** -0.5 * LOG2E, 1.0).astype(F32)[None, :]
    w_t = w_in.T
    proj = _matmul(xn, w_t, out_dtype=BF16, scale=scale, n_cols=n_main, w_is_nk=True,
                   name="nsa_in_proj")
    w_gate_t = jnp.pad(w_t[n_main:], ((0, LANES - n_gate), (0, 0)))
    gates = _matmul(xn, w_gate_t, out_dtype=BF16, scale=jnp.ones((1, LANES), F32), sigmoid=True,
                    w_is_nk=True, name="nsa_gate_proj")

    half = NSA_CMP_STRIDE
    nr = seq // half
    pos = jnp.stack([pos_k, pos_v])
    pa = pos[:, :half].reshape(2, 1, half * dh)
    pb = pos[:, half:].reshape(2, 1, half * dh)
    w1 = jnp.stack([w1_k, w1_v])
    w1a = w1[:, :half].reshape(2, half * dh, dh).astype(BF16)
    w1b = w1[:, half:].reshape(2, half * dh, dh).astype(BF16)
    w2 = jnp.stack([w2_k, w2_v]).astype(BF16)
    kvc = _compress(proj, q_cols, pa, pb, w1a, w1b, w2, batch=batch, seq=seq)

    wcol = q_cols + 2 * 2 * g_n * dh
    win = proj[:, wcol:wcol + 2 * g_n * dh].reshape(batch, seq, 2 * g_n * dh)
    win = jnp.pad(win, ((0, 0), (NSA_WINDOW, 0), (0, 0)))

    nb = seq // NSA_SEL_BLOCK
    cstart = jnp.arange(nr)[None, :] * NSA_CMP_STRIDE
    sstart = jnp.arange(nb)[:, None] * NSA_SEL_BLOCK
    ovl_t = ((cstart <= sstart + NSA_SEL_BLOCK - 1)
             & (cstart + NSA_CMP_LEN - 1 >= sstart)
             & (jnp.arange(nr)[None, :] < (seq - NSA_CMP_LEN) // NSA_CMP_STRIDE + 1)).astype(BF16)
    mix = _nsa_attention(_slope_pieces(), proj, gates, win, kvc, ovl_t, batch=batch, seq=seq)
    return _matmul(mix, w_out, out_dtype=F32, residual=h, name="nsa_out_proj")


def _moba_shared_kv(xn_kv, kv_w):
    one = jnp.ones((1, kv_w.shape[1]), F32)
    return _matmul(xn_kv, kv_w, out_dtype=BF16, scale=one, name="moba_kv_proj")


def _moba_layer(h, xn_q, kv, w_in, w_out, *, batch, seq):
    dh = HEAD_DIM
    q_cols = N_HEADS * dh
    scale = jnp.where(jnp.arange(w_in.shape[1]) < q_cols, dh ** -0.5 * LOG2E, 1.0).astype(F32)[None, :]
    qz = _matmul(xn_q, w_in, out_dtype=BF16, scale=scale, name="moba_in_proj")
    o = _moba_attention(_slope_pieces(), qz, kv, batch=batch, seq=seq)
    return _matmul(o, w_out, out_dtype=F32, residual=h, name="moba_out_proj")


def kernel(x, a_norm_g, a_w_in, a_cmp_pos_k, a_cmp_pos_v, a_cmp_w1_k, a_cmp_w2_k, a_cmp_w1_v, a_cmp_w2_v,
           a_w_out, kv_norm_g, kv_w, b_norm_g, b_w_in, b_w_out, final_norm_g):
    batch, seq, d = x.shape
    h = x.reshape(batch * seq, d)
    for layer in range(a_norm_g.shape[0]):
        h = _nsa_layer(h, a_norm_g[layer], a_w_in[layer], a_cmp_pos_k[layer], a_cmp_pos_v[layer],
                       a_cmp_w1_k[layer], a_cmp_w2_k[layer], a_cmp_w1_v[layer], a_cmp_w2_v[layer],
                       a_w_out[layer], batch=batch, seq=seq)
    kv = None
    for layer in range(b_norm_g.shape[0]):
        if layer == 0:
            xn_kv, xn_q = _rmsnorm(h, jnp.stack([kv_norm_g, b_norm_g[layer]]), BF16)
            kv = _moba_shared_kv(xn_kv, kv_w)
        else:
            (xn_q,) = _rmsnorm(h, b_norm_g[layer][None, :], BF16)
        h = _moba_layer(h, xn_q, kv, b_w_in[layer], b_w_out[layer], batch=batch, seq=seq)
    (out,) = _rmsnorm(h, final_norm_g[None, :], F32)
    return out.reshape(batch, seq, d)
```
